```python
import math
import jax, jax.numpy as jnp
from jax import lax
import numpy as np

D_MODEL = 2048
BATCH = 4
SEQ = 4096
DEPTH = 2

N_MIXERS = 2
N_RET_LAYERS = (DEPTH + 1) // 2
N_MOBA_LAYERS = DEPTH // 2
RET_HEADS = 8
RET_QK_DIM = D_MODEL // RET_HEADS
RET_V_WIDTH = 2 * D_MODEL
RET_V_DIM = RET_V_WIDTH // RET_HEADS
RET_CHUNK = 128
ROPE_BASE = 10000.0
ATT_HEADS = 16
ATT_HEAD_DIM = D_MODEL // ATT_HEADS
MOBA_BLOCK = 256
MOBA_TOPK = 3
MOBA_Q_BLOCK = 16
REL_BUCKETS = 32
REL_MAX_DIST = 128
D_FF = 256 * ((8 * D_MODEL // 3 + 255) // 256)
CONV_WIDTH = 3
RMS_EPS = 1e-6
GN_EPS = 1e-5
NEG_INF = -1e30

kernel_name = "hybrid_retention_moba_convffn"


def rms_norm(x, g):
    xf = x.astype(jnp.float32)
    y = xf * lax.rsqrt(jnp.mean(xf * xf, axis=-1, keepdims=True) + RMS_EPS)
    return (y * g.astype(jnp.float32)).astype(x.dtype)


def rotary(x, pos):
    d = x.shape[-1]
    half = d // 2
    inv = ROPE_BASE ** (-jnp.arange(half, dtype=jnp.float32) / half)
    ang = pos.astype(jnp.float32)[:, None] * inv[None, :]
    cos = jnp.cos(ang)[None, :, None, :]
    sin = jnp.sin(ang)[None, :, None, :]
    xf = x.astype(jnp.float32)
    x1, x2 = xf[..., :half], xf[..., half:]
    return jnp.concatenate([x1 * cos - x2 * sin, x2 * cos + x1 * sin], axis=-1).astype(x.dtype)


def retention_mixer(h, w_in, gn_gain, w_out):
    B, S, _ = h.shape
    H, dk, dv, C = RET_HEADS, RET_QK_DIM, RET_V_DIM, RET_CHUNK
    nc = S // C
    proj = h @ w_in
    q, k, v, g = jnp.split(proj, [D_MODEL, 2 * D_MODEL, 2 * D_MODEL + RET_V_WIDTH], axis=-1)
    pos = jnp.arange(S)
    q = rotary(q.reshape(B, S, H, dk), pos)
    k = rotary(k.reshape(B, S, H, dk), pos) * (dk ** -0.5)
    v = v.reshape(B, S, H, dv)
    dt = q.dtype
    qc = q.reshape(B, nc, C, H, dk).transpose(0, 3, 1, 2, 4)
    kc = k.reshape(B, nc, C, H, dk).transpose(0, 3, 1, 2, 4)
    vc = v.reshape(B, nc, C, H, dv).transpose(0, 3, 1, 2, 4)
    log_gamma = jnp.log1p(-jnp.power(2.0, -5.0 - jnp.arange(H, dtype=jnp.float32)))
    idx = jnp.arange(C, dtype=jnp.float32)
    diff = idx[:, None] - idx[None, :]
    decay_in = jnp.where(diff >= 0, jnp.exp(log_gamma[:, None, None] * jnp.maximum(diff, 0.0)), 0.0).astype(dt)
    xi = jnp.exp(log_gamma[:, None] * (idx + 1.0)).astype(dt)
    zeta = jnp.exp(log_gamma[:, None] * (C - 1.0 - idx)).astype(dt)
    chunk_decay = jnp.exp(log_gamma * C).astype(dt)
    inner = jnp.einsum('bhncd,bhnmd->bhncm', qc, kc) * decay_in[None, :, None]
    inner_out = jnp.einsum('bhncm,bhnme->bhnce', inner, vc)

    def step(state, inp):
        q_i, k_i, v_i = inp
        cross = jnp.einsum('bhcd,bhde->bhce', q_i, state) * xi[None, :, :, None]
        state = state * chunk_decay[None, :, None, None] + jnp.einsum(
            'bhcd,bhce->bhde', k_i, v_i * zeta[None, :, :, None])
        return state, cross

    state0 = jnp.zeros((B, H, dk, dv), dt)
    _, cross = lax.scan(step, state0, (qc.transpose(2, 0, 1, 3, 4),
                                       kc.transpose(2, 0, 1, 3, 4),
                                       vc.transpose(2, 0, 1, 3, 4)))
    o = inner_out + cross.transpose(1, 2, 0, 3, 4)
    o = o.transpose(0, 2, 3, 1, 4).reshape(B, S, H, dv)
    of = o.astype(jnp.float32)
    mu = jnp.mean(of, axis=-1, keepdims=True)
    var = jnp.mean(jnp.square(of - mu), axis=-1, keepdims=True)
    on = ((of - mu) * lax.rsqrt(var + GN_EPS)).reshape(B, S, RET_V_WIDTH) * gn_gain.astype(jnp.float32)
    y = (jax.nn.silu(g.astype(jnp.float32)) * on).astype(h.dtype)
    return y @ w_out


def t5_bucket(rel):
    n = jnp.maximum(rel, 0)
    max_exact = REL_BUCKETS // 2
    nf = jnp.maximum(n, max_exact).astype(jnp.float32)
    large = max_exact + (jnp.log(nf / max_exact) / math.log(REL_MAX_DIST / max_exact)
                         * (REL_BUCKETS - max_exact)).astype(jnp.int32)
    large = jnp.minimum(large, REL_BUCKETS - 1)
    return jnp.where(n < max_exact, n, large)


def moba_mixer(h, w_qkv, w_out, rel_bias):
    B, S, _ = h.shape
    H, Dh, Bk, QB = ATT_HEADS, ATT_HEAD_DIM, MOBA_BLOCK, MOBA_Q_BLOCK
    nb = -(-S // Bk)
    pad = nb * Bk - S
    q, k, v = jnp.split(h @ w_qkv, 3, axis=-1)
    q = q.reshape(B, S, H, Dh).transpose(0, 2, 1, 3)
    k = k.reshape(B, S, H, Dh).transpose(0, 2, 1, 3)
    v = v.reshape(B, S, H, Dh).transpose(0, 2, 1, 3)
    kb = jnp.pad(k, ((0, 0), (0, 0), (0, pad), (0, 0))).reshape(B, H, nb, Bk, Dh)
    vb = jnp.pad(v, ((0, 0), (0, 0), (0, pad), (0, 0))).reshape(B, H, nb, Bk, Dh)
    k_mean = jnp.mean(kb.astype(jnp.float32), axis=3)
    gate = jnp.einsum('bhsd,bhnd->bhsn', q.astype(jnp.float32), k_mean)
    q_blk = jnp.arange(S) // Bk
    past = jnp.arange(nb)[None, :] < q_blk[:, None]
    gate = jnp.where(past[None, None], gate, NEG_INF)
    topk = min(MOBA_TOPK, nb)
    _, sel = lax.top_k(gate, topk)
    sel_ok = sel < q_blk[None, None, :, None]
    tbl = rel_bias.astype(jnp.float32).T
    scale = Dh ** -0.5
    b_idx = jnp.arange(B)[:, None, None]
    h_idx = jnp.arange(H)[None, :, None]
    key_off = jnp.arange(Bk)

    def one_block(c):
        q0 = c * QB
        q_pos = q0 + jnp.arange(QB)
        qc = lax.dynamic_slice_in_dim(q, q0, QB, axis=2)
        sel_c = lax.dynamic_slice_in_dim(sel, q0, QB, axis=2)
        ok_c = lax.dynamic_slice_in_dim(sel_ok, q0, QB, axis=2)
        own = q0 // Bk
        k_own = lax.dynamic_index_in_dim(kb, own, axis=2, keepdims=False)
        v_own = lax.dynamic_index_in_dim(vb, own, axis=2, keepdims=False)
        rel_own = q_pos[:, None] - (own * Bk + key_off)[None, :]
        l_own = (jnp.einsum('bhqd,bhkd->bhqk', qc, k_own).astype(jnp.float32) * scale
                 + tbl[:, t5_bucket(rel_own)][None])
        l_own = jnp.where(rel_own[None, None] >= 0, l_own, NEG_INF)
        flat = sel_c.reshape(B, H, QB * topk)
        k_sel = kb[b_idx, h_idx, flat].reshape(B, H, QB, topk * Bk, Dh)
        v_sel = vb[b_idx, h_idx, flat].reshape(B, H, QB, topk * Bk, Dh)
        sel_pos = (sel_c[..., None] * Bk + key_off).reshape(B, H, QB, topk * Bk)
        rel_sel = q_pos[None, None, :, None] - sel_pos
        l_sel = (jnp.einsum('bhqd,bhqkd->bhqk', qc, k_sel).astype(jnp.float32) * scale
                 + tbl[h_idx[..., None], t5_bucket(rel_sel)])
        ok = jnp.repeat(ok_c, Bk, axis=-1, total_repeat_length=topk * Bk)
        l_sel = jnp.where(ok, l_sel, NEG_INF)
        p = jax.nn.softmax(jnp.concatenate([l_own, l_sel], axis=-1), axis=-1).astype(v.dtype)
        return (jnp.einsum('bhqk,bhkd->bhqd', p[..., :Bk], v_own)
                + jnp.einsum('bhqk,bhqkd->bhqd', p[..., Bk:], v_sel))

    o = lax.map(one_block, jnp.arange(S // QB))
    o = o.transpose(1, 0, 3, 2, 4).reshape(B, S, H * Dh)
    return o @ w_out


def conv_ffn(h, w_up, conv_w, conv_b, w_down):
    u = h @ w_up
    ch = u.shape[-1]
    u = lax.conv_general_dilated(
        u, conv_w.astype(u.dtype)[:, None, :], window_strides=(1,),
        padding=[(CONV_WIDTH - 1, 0)], dimension_numbers=('NWC', 'WIO', 'NWC'),
        feature_group_count=ch) + conv_b
    gate, val = jnp.split(u, 2, axis=-1)
    return (jax.nn.silu(gate) * val) @ w_down


def setup_inputs(seed: int = 0) -> dict:
    key = jax.random.key(seed)
    ks = jax.random.split(key, 16)
    f32 = jnp.float32

    def w(k, shape, fan_in):
        return jax.random.normal(k, shape, f32) * (fan_in ** -0.5)

    def gain(k, shape):
        return 1.0 + 0.02 * jax.random.normal(k, shape, f32)

    return {
        "x": jax.random.normal(ks[0], (BATCH, SEQ, D_MODEL), f32),
        "mix_norm": gain(ks[1], (DEPTH, D_MODEL)),
        "ret_w_in": w(ks[2], (N_RET_LAYERS, D_MODEL, 2 * D_MODEL + 2 * RET_V_WIDTH), D_MODEL),
        "ret_gn": gain(ks[3], (N_RET_LAYERS, RET_V_WIDTH)),
        "ret_w_out": w(ks[4], (N_RET_LAYERS, RET_V_WIDTH, D_MODEL), RET_V_WIDTH),
        "moba_w_qkv": w(ks[5], (N_MOBA_LAYERS, D_MODEL, 3 * D_MODEL), D_MODEL),
        "moba_w_out": w(ks[6], (N_MOBA_LAYERS, D_MODEL, D_MODEL), D_MODEL),
        "rel_bias": 0.5 * jax.random.normal(ks[7], (REL_BUCKETS, ATT_HEADS), f32),
        "ffn_norm": gain(ks[8], (DEPTH, D_MODEL)),
        "ffn_w_up": w(ks[9], (DEPTH, D_MODEL, 2 * D_FF), D_MODEL),
        "ffn_conv_w": w(ks[10], (DEPTH, CONV_WIDTH, 2 * D_FF), CONV_WIDTH),
        "ffn_conv_b": 0.02 * jax.random.normal(ks[11], (DEPTH, 2 * D_FF), f32),
        "ffn_w_down": w(ks[12], (DEPTH, D_FF, D_MODEL), D_FF),
        "final_norm": gain(ks[13], (D_MODEL,)),
    }


def reference(x, mix_norm, ret_w_in, ret_gn, ret_w_out, moba_w_qkv, moba_w_out, rel_bias,
              ffn_norm, ffn_w_up, ffn_conv_w, ffn_conv_b, ffn_w_down, final_norm):
    h = x
    for i in range(DEPTH):
        hn = rms_norm(h, mix_norm[i])
        j = i // N_MIXERS
        if i % N_MIXERS == 0:
            h = h + retention_mixer(hn, ret_w_in[j], ret_gn[j], ret_w_out[j])
        else:
            h = h + moba_mixer(hn, moba_w_qkv[j], moba_w_out[j], rel_bias)
        h = h + conv_ffn(rms_norm(h, ffn_norm[i]), ffn_w_up[i], ffn_conv_w[i], ffn_conv_b[i], ffn_w_down[i])
    return rms_norm(h, final_norm)
```

```python
import functools
import math

import numpy as np
import jax
import jax.numpy as jnp
from jax import lax
from jax.experimental import pallas as pl
from jax.experimental.pallas import tpu as pltpu

F32 = jnp.float32
BF16 = jnp.bfloat16

N_MIXERS = 2
RET_HEADS = 8
ROPE_BASE = 10000.0
MOBA_BLOCK = 256
MOBA_TOPK = 3
REL_MAX_DIST = 128
CONV_WIDTH = 3
RMS_EPS = 1e-6
GN_EPS = 1e-5
NEG_INF = -1e30

RET_CHUNK = 256
RET_ROWS_PER_STEP = 512

V7X_SUBLANES = 8
V7X_LANES = 128
V7X_VMEM_LIMIT_BYTES = 56 * 1024 * 1024

_NT = (((1,), (1,)), ((), ()))
_TN = (((0,), (0,)), ((), ()))


def _params(*semantics):
    return pltpu.CompilerParams(dimension_semantics=semantics,
                                vmem_limit_bytes=V7X_VMEM_LIMIT_BYTES)


def _sigmoid(x):
    return 1.0 / (1.0 + jnp.exp(-x))


def _rmsnorm_kernel(x_ref, g_ref, o_ref):
    x = x_ref[...]
    ms = jnp.mean(x * x, axis=-1, keepdims=True)
    o_ref[...] = (x * lax.rsqrt(ms + RMS_EPS) * g_ref[...]).astype(o_ref.dtype)


def _rmsnorm(x, g, out_dtype, tm=512):
    m, d = x.shape
    return pl.pallas_call(
        _rmsnorm_kernel,
        grid=(m // tm,),
        in_specs=[pl.BlockSpec((tm, d), lambda i: (i, 0)),
                  pl.BlockSpec((1, d), lambda i: (0, 0))],
        out_specs=pl.BlockSpec((tm, d), lambda i: (i, 0)),
        out_shape=jax.ShapeDtypeStruct((m, d), out_dtype),
        compiler_params=_params("arbitrary"),
        name="rmsnorm",
    )(x, g.reshape(1, d).astype(F32))


def _matmul_kernel(a_ref, w_ref, o_ref):
    o_ref[...] = jnp.dot(a_ref[...], w_ref[...],
                         preferred_element_type=F32).astype(o_ref.dtype)


def _matmul_residual_kernel(a_ref, w_ref, r_ref, o_ref):
    o_ref[...] = (r_ref[...] + jnp.dot(a_ref[...], w_ref[...],
                                       preferred_element_type=F32)).astype(o_ref.dtype)


def _matmul(a, w, out_dtype, residual=None, *, tm, tn, name):
    m, k = a.shape
    n = w.shape[1]
    in_specs = [pl.BlockSpec((tm, k), lambda i, j: (i, 0)),
                pl.BlockSpec((k, tn), lambda i, j: (0, j))]
    args = [a, w]
    body = _matmul_kernel
    if residual is not None:
        in_specs.append(pl.BlockSpec((tm, tn), lambda i, j: (i, j)))
        args.append(residual)
        body = _matmul_residual_kernel
    return pl.pallas_call(
        body,
        grid=(m // tm, n // tn),
        in_specs=in_specs,
        out_specs=pl.BlockSpec((tm, tn), lambda i, j: (i, j)),
        out_shape=jax.ShapeDtypeStruct((m, n), out_dtype),
        compiler_params=_params("arbitrary", "arbitrary"),
        name=name,
    )(*args)


def _retention_kernel(lg_ref, q_ref, k_ref, v_ref, g_ref, cos_ref, sin_ref, gain_ref, o_ref,
                      state_ref, decay_ref, xi_ref, zeta_ref, *, chunk, n_chunks):
    h = pl.program_id(1)
    dk = q_ref.shape[1]
    dv = v_ref.shape[1]
    log_gamma = lg_ref[h]

    @pl.when(pl.program_id(2) == 0)
    def _start_of_sequence():
        state_ref[...] = jnp.zeros_like(state_ref)
        r = lax.broadcasted_iota(jnp.int32, (chunk, chunk), 0)
        c = lax.broadcasted_iota(jnp.int32, (chunk, chunk), 1)
        diff = (r - c).astype(F32)
        decay_ref[...] = jnp.where(diff >= 0, jnp.exp(log_gamma * jnp.maximum(diff, 0.0)), 0.0)
        idx = lax.broadcasted_iota(jnp.int32, (chunk, dv), 0).astype(F32)
        xi_ref[...] = jnp.exp(log_gamma * (idx + 1.0))
        zeta_ref[...] = jnp.exp(log_gamma * (chunk - 1.0 - idx))

    def rope(x, cos, sin_signed):
        xf = x.astype(F32)
        return xf * cos + pltpu.roll(xf, dk // 2, axis=1) * sin_signed

    for ci in range(n_chunks):
        rows = pl.ds(ci * chunk, chunk)
        cos = cos_ref[rows, :]
        sin_signed = sin_ref[rows, :]
        q = rope(q_ref[rows, :], cos, sin_signed).astype(BF16)
        k = (rope(k_ref[rows, :], cos, sin_signed) * (dk ** -0.5)).astype(BF16)
        v = v_ref[rows, :]
        scores = lax.dot_general(q, k, _NT, preferred_element_type=F32) * decay_ref[...]
        inner = jnp.dot(scores.astype(BF16), v, preferred_element_type=F32)
        state = state_ref[...]
        cross = jnp.dot(q, state.astype(BF16), preferred_element_type=F32) * xi_ref[...]
        v_decayed = (v.astype(F32) * zeta_ref[...]).astype(BF16)
        chunk_decay = xi_ref[chunk - 1:chunk, :]
        state_ref[...] = state * chunk_decay + lax.dot_general(
            k, v_decayed, _TN, preferred_element_type=F32)
        o = inner + cross
        mu = jnp.mean(o, axis=-1, keepdims=True)
        d = o - mu
        var = jnp.mean(d * d, axis=-1, keepdims=True)
        normed = d * lax.rsqrt(var + GN_EPS) * gain_ref[...]
        gate = g_ref[rows, :].astype(F32)
        o_ref[rows, :] = (gate * _sigmoid(gate) * normed).astype(o_ref.dtype)


def _retention_core(proj, gn_gain, batch, seq):
    m, width = proj.shape
    heads = RET_HEADS
    vwidth = gn_gain.shape[0]
    d_model = (width - 2 * vwidth) // 2
    dk = d_model // heads
    dv = vwidth // heads
    rows = RET_ROWS_PER_STEP
    chunk = RET_CHUNK
    steps = seq // rows
    assert seq % rows == 0 and rows % chunk == 0 and dk % (2 * V7X_LANES) == 0 and dv % V7X_LANES == 0

    half = dk // 2
    inv = ROPE_BASE ** (-jnp.arange(half, dtype=F32) / half)
    ang = jnp.arange(seq).astype(F32)[:, None] * inv[None, :]
    cos = jnp.concatenate([jnp.cos(ang), jnp.cos(ang)], axis=-1)
    sin_signed = jnp.concatenate([-jnp.sin(ang), jnp.sin(ang)], axis=-1)
    log_gamma = jnp.log1p(-jnp.power(2.0, -5.0 - jnp.arange(heads, dtype=F32)))

    k_off = d_model // dk
    v_off = 2 * d_model // dv
    g_off = (2 * d_model + vwidth) // dv
    row_map = lambda b, h, t: b * steps + t
    return pl.pallas_call(
        functools.partial(_retention_kernel, chunk=chunk, n_chunks=rows // chunk),
        grid=(batch, heads, steps),
        in_specs=[
            pl.BlockSpec(memory_space=pltpu.SMEM),
            pl.BlockSpec((rows, dk), lambda b, h, t: (row_map(b, h, t), h)),
            pl.BlockSpec((rows, dk), lambda b, h, t: (row_map(b, h, t), k_off + h)),
            pl.BlockSpec((rows, dv), lambda b, h, t: (row_map(b, h, t), v_off + h)),
            pl.BlockSpec((rows, dv), lambda b, h, t: (row_map(b, h, t), g_off + h)),
            pl.BlockSpec((rows, dk), lambda b, h, t: (t, 0)),
            pl.BlockSpec((rows, dk), lambda b, h, t: (t, 0)),
            pl.BlockSpec((1, dv), lambda b, h, t: (0, h)),
        ],
        out_specs=pl.BlockSpec((rows, dv), lambda b, h, t: (row_map(b, h, t), h)),
        out_shape=jax.ShapeDtypeStruct((m, vwidth), BF16),
        scratch_shapes=[pltpu.VMEM((dk, dv), F32),
                        pltpu.VMEM((chunk, chunk), F32),
                        pltpu.VMEM((chunk, dv), F32),
                        pltpu.VMEM((chunk, dv), F32)],
        compiler_params=_params("arbitrary", "arbitrary", "arbitrary"),
        name="retention_core",
    )(log_gamma, proj, proj, proj, proj, cos, sin_signed, gn_gain.reshape(1, vwidth).astype(F32))


def _t5_bucket_table(n_rel, n_buckets, max_dist):
    n = np.arange(n_rel)
    max_exact = n_buckets // 2
    nf = np.maximum(n, max_exact).astype(np.float64)
    large = max_exact + (np.log(nf / max_exact) / math.log(max_dist / max_exact)
                         * (n_buckets - max_exact)).astype(np.int64)
    large = np.minimum(large, n_buckets - 1)
    return np.where(n < max_exact, n, large).astype(np.int32)


def _moba_kernel(tbl_ref, q_ref, k_ref, vt_ref, bucket_own_ref, bucket_prev_ref, o_ref,
                 kmean_ref, bias_own_ref, bias_prev_ref, sel_ref, m_ref, l_ref, acc_ref,
                 *, n_blocks, blk, topk, n_buckets, far_bucket, scale):
    h = pl.program_id(1)
    qb = pl.program_id(2)

    @pl.when(qb == 0)
    def _start_of_head():
        for j in range(n_blocks):
            kj = k_ref[pl.ds(j * blk, blk), :].astype(F32)
            kmean_ref[pl.ds(j, 1), :] = jnp.mean(kj, axis=0, keepdims=True)
        bucket_own = bucket_own_ref[...]
        bucket_prev = bucket_prev_ref[...]
        bias_own = jnp.zeros((blk, blk), F32)
        bias_prev = jnp.zeros((blk, blk), F32)
        for b in range(n_buckets):
            t = tbl_ref[h, b]
            bias_own = jnp.where(bucket_own == b, t, bias_own)
            bias_prev = jnp.where(bucket_prev == b, t, bias_prev)
        bias_own_ref[...] = bias_own
        bias_prev_ref[...] = bias_prev

    q = q_ref[...]

    gate = lax.dot_general(kmean_ref[...], q.astype(F32), _NT,
                           precision=lax.Precision.HIGHEST, preferred_element_type=F32)
    block_id = lax.broadcasted_iota(jnp.int32, (n_blocks, blk), 0)
    past = block_id < qb
    gate = jnp.where(past, gate, NEG_INF)
    sel = jnp.zeros((n_blocks, blk), F32)
    for _ in range(topk):
        best = jnp.max(gate, axis=0, keepdims=True)
        first = jnp.min(jnp.where(gate == best, block_id, n_blocks), axis=0, keepdims=True)
        pick = block_id == first
        sel = jnp.where(pick, 1.0, sel)
        gate = jnp.where(pick, -jnp.inf, gate)
    sel_ref[...] = jnp.where(past, sel, 0.0)

    def scores_t(j):
        kj = k_ref[pl.ds(pl.multiple_of(j * blk, blk), blk), :]
        return lax.dot_general(kj, q, _NT, preferred_element_type=F32) * scale

    key_pos = lax.broadcasted_iota(jnp.int32, (blk, blk), 0)
    query_pos = lax.broadcasted_iota(jnp.int32, (blk, blk), 1)
    s = jnp.where(key_pos <= query_pos, scores_t(qb) + bias_own_ref[...], NEG_INF)
    m0 = jnp.max(s, axis=0, keepdims=True)
    p = jnp.exp(s - m0)
    m_ref[...] = m0
    l_ref[...] = jnp.sum(p, axis=0, keepdims=True)
    acc_ref[...] = jnp.dot(vt_ref[qb], p.astype(BF16), preferred_element_type=F32)

    def attend_past(j, bias):
        chosen = sel_ref[pl.ds(j, 1), :] > 0.5
        s = jnp.where(chosen, scores_t(j) + bias, NEG_INF)
        m_old = m_ref[...]
        m_new = jnp.maximum(m_old, jnp.max(s, axis=0, keepdims=True))
        alpha = jnp.exp(m_old - m_new)
        p = jnp.exp(s - m_new)
        l_ref[...] = alpha * l_ref[...] + jnp.sum(p, axis=0, keepdims=True)
        acc_ref[...] = alpha * acc_ref[...] + jnp.dot(vt_ref[j], p.astype(BF16),
                                                      preferred_element_type=F32)
        m_ref[...] = m_new

    @pl.when(qb >= 1)
    def _previous_block():
        attend_past(qb - 1, bias_prev_ref[...])

    far_bias = tbl_ref[h, far_bucket]

    def far_block(j, carry):
        attend_past(j, far_bias)
        return carry

    lax.fori_loop(0, qb - 1, far_block, 0)

    o_ref[...] = (acc_ref[...] / l_ref[...]).T.astype(o_ref.dtype)


def _moba_core(qkv, rel_bias, batch, seq):
    m, width = qkv.shape
    d_model = width // 3
    n_buckets, heads = rel_bias.shape
    dh = d_model // heads
    blk = MOBA_BLOCK
    n_blocks = seq // blk
    assert seq % blk == 0 and dh % V7X_LANES == 0

    buckets = _t5_bucket_table(max(seq, 2 * blk), n_buckets, REL_MAX_DIST)
    far_bucket = int(buckets[blk + 1])
    assert np.all(buckets[blk + 1:] == far_bucket), "blocks two or more back must share one bucket"
    rel_own = np.arange(blk)[None, :] - np.arange(blk)[:, None]
    bucket_own = buckets[np.maximum(rel_own, 0)]
    bucket_prev = buckets[rel_own + blk]

    v = qkv[:, 2 * d_model:].reshape(batch, n_blocks, blk, heads, dh)
    vt = jnp.transpose(v, (0, 3, 1, 4, 2))

    return pl.pallas_call(
        functools.partial(_moba_kernel, n_blocks=n_blocks, blk=blk, topk=min(MOBA_TOPK, n_blocks),
                          n_buckets=n_buckets, far_bucket=far_bucket, scale=dh ** -0.5),
        grid=(batch, heads, n_blocks),
        in_specs=[
            pl.BlockSpec(memory_space=pltpu.SMEM),
            pl.BlockSpec((blk, dh), lambda b, h, i: (b * n_blocks + i, h)),
            pl.BlockSpec((seq, dh), lambda b, h, i: (b, heads + h)),
            pl.BlockSpec((None, None, n_blocks, dh, blk), lambda b, h, i: (b, h, 0, 0, 0)),
            pl.BlockSpec((blk, blk), lambda b, h, i: (0, 0)),
            pl.BlockSpec((blk, blk), lambda b, h, i: (0, 0)),
        ],
        out_specs=pl.BlockSpec((blk, dh), lambda b, h, i: (b * n_blocks + i, h)),
        out_shape=jax.ShapeDtypeStruct((m, d_model), BF16),
        scratch_shapes=[pltpu.VMEM((n_blocks, dh), F32),
                        pltpu.VMEM((blk, blk), F32),
                        pltpu.VMEM((blk, blk), F32),
                        pltpu.VMEM((n_blocks, blk), F32),
                        pltpu.VMEM((1, blk), F32),
                        pltpu.VMEM((1, blk), F32),
                        pltpu.VMEM((dh, blk), F32)],
        compiler_params=_params("arbitrary", "arbitrary", "arbitrary"),
        name="moba_core",
    )(rel_bias.T.astype(F32), qkv, qkv, vt, jnp.asarray(bucket_own), jnp.asarray(bucket_prev))


def _ffn_up_kernel(x_ref, wg_ref, wv_ref, cwg_ref, cwv_ref, cbg_ref, cbv_ref, o_ref, carry_ref,
                   *, tiles_per_seq):
    @pl.when(pl.program_id(1) % tiles_per_seq == 0)
    def _start_of_sequence():
        carry_ref[...] = jnp.zeros_like(carry_ref)

    x = x_ref[...]
    tm = x.shape[0]
    tn = o_ref.shape[1]
    row = lax.broadcasted_iota(jnp.int32, (tm, tn), 0)

    def conv_branch(w_ref, cw_ref, cb_ref, slot):
        u = jnp.dot(x, w_ref[...], preferred_element_type=F32)
        tail = carry_ref[slot]
        prev1 = tail[V7X_SUBLANES - 1:V7X_SUBLANES, :]
        prev2 = tail[V7X_SUBLANES - 2:V7X_SUBLANES - 1, :]
        back1 = jnp.where(row == 0, prev1, pltpu.roll(u, 1, axis=0))
        back2 = jnp.where(row == 0, prev2, jnp.where(row == 1, prev1, pltpu.roll(u, 2, axis=0)))
        carry_ref[slot] = u[tm - V7X_SUBLANES:, :]
        cw = cw_ref[...]
        return cw[2:3, :] * u + cw[1:2, :] * back1 + cw[0:1, :] * back2 + cb_ref[...]

    gate = conv_branch(wg_ref, cwg_ref, cbg_ref, 0)
    val = conv_branch(wv_ref, cwv_ref, cbv_ref, 1)
    o_ref[...] = (gate * _sigmoid(gate) * val).astype(o_ref.dtype)


def _ffn_up(x, w_up, conv_w, conv_b, seq, *, tm, tn):
    m, k = x.shape
    d_ff = w_up.shape[1] // 2
    n_col = d_ff // tn
    assert CONV_WIDTH == 3 and seq % tm == 0 and d_ff % tn == 0
    conv_b = conv_b.reshape(1, 2 * d_ff).astype(F32)
    conv_w = conv_w.astype(F32)
    return pl.pallas_call(
        functools.partial(_ffn_up_kernel, tiles_per_seq=seq // tm),
        grid=(n_col, m // tm),
        in_specs=[
            pl.BlockSpec((tm, k), lambda j, i: (i, 0)),
            pl.BlockSpec((k, tn), lambda j, i: (0, j)),
            pl.BlockSpec((k, tn), lambda j, i: (0, n_col + j)),
            pl.BlockSpec((CONV_WIDTH, tn), lambda j, i: (0, j)),
            pl.BlockSpec((CONV_WIDTH, tn), lambda j, i: (0, n_col + j)),
            pl.BlockSpec((1, tn), lambda j, i: (0, j)),
            pl.BlockSpec((1, tn), lambda j, i: (0, n_col + j)),
        ],
        out_specs=pl.BlockSpec((tm, tn), lambda j, i: (i, j)),
        out_shape=jax.ShapeDtypeStruct((m, d_ff), BF16),
        scratch_shapes=[pltpu.VMEM((2, V7X_SUBLANES, tn), F32)],
        compiler_params=_params("arbitrary", "arbitrary"),
        name="ffn_up_conv_gate",
    )(x, w_up, w_up, conv_w, conv_w, conv_b, conv_b)


def kernel(x, mix_norm, ret_w_in, ret_gn, ret_w_out, moba_w_qkv, moba_w_out, rel_bias,
           ffn_norm, ffn_w_up, ffn_conv_w, ffn_conv_b, ffn_w_down, final_norm):
    batch, seq, d_model = x.shape
    depth = mix_norm.shape[0]
    h = x.reshape(batch * seq, d_model)
    for i in range(depth):
        hn = _rmsnorm(h, mix_norm[i], BF16)
        j = i // N_MIXERS
        if i % N_MIXERS == 0:
            proj = _matmul(hn, ret_w_in[j].astype(BF16), BF16, tm=1024, tn=1024, name="ret_in_proj")
            y = _retention_core(proj, ret_gn[j], batch, seq)
            h = _matmul(y, ret_w_out[j].astype(BF16), F32, h, tm=512, tn=1024, name="ret_out_proj")
        else:
            qkv = _matmul(hn, moba_w_qkv[j].astype(BF16), BF16, tm=1024, tn=1024, name="moba_qkv_proj")
            o = _moba_core(qkv, rel_bias, batch, seq)
            h = _matmul(o, moba_w_out[j].astype(BF16), F32, h, tm=1024, tn=1024, name="moba_out_proj")
        hn = _rmsnorm(h, ffn_norm[i], BF16)
        a = _ffn_up(hn, ffn_w_up[i].astype(BF16), ffn_conv_w[i], ffn_conv_b[i], seq, tm=1024, tn=512)
        h = _matmul(a, ffn_w_down[i].astype(BF16), F32, h, tm=512, tn=512, name="ffn_down_proj")
    return _rmsnorm(h, final_norm, F32).reshape(batch, seq, d_model)
```

```python
import functools
import math

import numpy as np
import jax
import jax.numpy as jnp
from jax import lax
from jax.experimental import pallas as pl
from jax.experimental.pallas import tpu as pltpu

F32 = jnp.float32
BF16 = jnp.bfloat16

N_MIXERS = 2
RET_HEADS = 8
ROPE_BASE = 10000.0
MOBA_BLOCK = 256
MOBA_TOPK = 3
REL_MAX_DIST = 128
CONV_WIDTH = 3
RMS_EPS = 1e-6
GN_EPS = 1e-5
NEG_INF = -1e30

RET_CHUNK = 256
RET_ROWS_PER_STEP = 512
MOBA_HEADS_PER_STEP = 4

V7X_SUBLANES = 8
V7X_LANES = 128
V7X_VMEM_LIMIT_BYTES = 56 * 1024 * 1024

_NT = (((1,), (1,)), ((), ()))
_TN = (((0,), (0,)), ((), ()))


def _params(*semantics):
    return pltpu.CompilerParams(dimension_semantics=semantics,
                                vmem_limit_bytes=V7X_VMEM_LIMIT_BYTES)


def _sigmoid(x):
    return 1.0 / (1.0 + jnp.exp(-x))


def _rmsnorm_kernel(x_ref, g_ref, o_ref):
    x = x_ref[...]
    ms = jnp.mean(x * x, axis=-1, keepdims=True)
    o_ref[...] = (x * lax.rsqrt(ms + RMS_EPS) * g_ref[...]).astype(o_ref.dtype)


def _rmsnorm(x, g, out_dtype, tm=512):
    m, d = x.shape
    return pl.pallas_call(
        _rmsnorm_kernel,
        grid=(m // tm,),
        in_specs=[pl.BlockSpec((tm, d), lambda i: (i, 0)),
                  pl.BlockSpec((1, d), lambda i: (0, 0))],
        out_specs=pl.BlockSpec((tm, d), lambda i: (i, 0)),
        out_shape=jax.ShapeDtypeStruct((m, d), out_dtype),
        compiler_params=_params("arbitrary"),
        name="rmsnorm",
    )(x, g.reshape(1, d).astype(F32))


def _cast_weight_tile(w_ref, wb_ref):
    @pl.when(pl.program_id(1) == 0)
    def _():
        wb_ref[...] = w_ref[...].astype(BF16)


def _matmul_kernel(a_ref, w_ref, o_ref, wb_ref):
    _cast_weight_tile(w_ref, wb_ref)
    o_ref[...] = jnp.dot(a_ref[...], wb_ref[...],
                         preferred_element_type=F32).astype(o_ref.dtype)


def _matmul_residual_kernel(a_ref, w_ref, r_ref, o_ref, wb_ref):
    _cast_weight_tile(w_ref, wb_ref)
    o_ref[...] = (r_ref[...] + jnp.dot(a_ref[...], wb_ref[...],
                                       preferred_element_type=F32)).astype(o_ref.dtype)


def _matmul(a, w, layer, out_dtype, residual=None, *, tm, tn, name):
    m, k = a.shape
    n = w.shape[2]
    in_specs = [pl.BlockSpec((tm, k), lambda j, i: (i, 0)),
                pl.BlockSpec((None, k, tn), lambda j, i: (layer, 0, j))]
    args = [a, w]
    body = _matmul_kernel
    if residual is not None:
        in_specs.append(pl.BlockSpec((tm, tn), lambda j, i: (i, j)))
        args.append(residual)
        body = _matmul_residual_kernel
    return pl.pallas_call(
        body,
        grid=(n // tn, m // tm),
        in_specs=in_specs,
        out_specs=pl.BlockSpec((tm, tn), lambda j, i: (i, j)),
        out_shape=jax.ShapeDtypeStruct((m, n), out_dtype),
        scratch_shapes=[pltpu.VMEM((k, tn), BF16)],
        compiler_params=_params("arbitrary", "arbitrary"),
        name=name,
    )(*args)


def _retention_kernel(lg_ref, q_ref, k_ref, v_ref, g_ref, cos_ref, sin_ref, gain_ref, o_ref,
                      state_ref, decay_ref, xi_ref, zeta_ref, *, chunk, n_chunks):
    h = pl.program_id(1)
    dk = q_ref.shape[1]
    dv = v_ref.shape[1]
    log_gamma = lg_ref[h]

    @pl.when(pl.program_id(2) == 0)
    def _start_of_sequence():
        state_ref[...] = jnp.zeros_like(state_ref)
        r = lax.broadcasted_iota(jnp.int32, (chunk, chunk), 0)
        c = lax.broadcasted_iota(jnp.int32, (chunk, chunk), 1)
        diff = (r - c).astype(F32)
        decay_ref[...] = jnp.where(diff >= 0, jnp.exp(log_gamma * jnp.maximum(diff, 0.0)), 0.0)
        idx = lax.broadcasted_iota(jnp.int32, (chunk, dv), 0).astype(F32)
        xi_ref[...] = jnp.exp(log_gamma * (idx + 1.0))
        zeta_ref[...] = jnp.exp(log_gamma * (chunk - 1.0 - idx))

    def rope(x, cos, sin_signed):
        xf = x.astype(F32)
        return xf * cos + pltpu.roll(xf, dk // 2, axis=1) * sin_signed

    for ci in range(n_chunks):
        rows = pl.ds(ci * chunk, chunk)
        cos = cos_ref[rows, :]
        sin_signed = sin_ref[rows, :]
        q = rope(q_ref[rows, :], cos, sin_signed).astype(BF16)
        k = (rope(k_ref[rows, :], cos, sin_signed) * (dk ** -0.5)).astype(BF16)
        v = v_ref[rows, :]
        scores = lax.dot_general(q, k, _NT, preferred_element_type=F32) * decay_ref[...]
        inner = jnp.dot(scores.astype(BF16), v, preferred_element_type=F32)
        state = state_ref[...]
        cross = jnp.dot(q, state.astype(BF16), preferred_element_type=F32) * xi_ref[...]
        v_decayed = (v.astype(F32) * zeta_ref[...]).astype(BF16)
        chunk_decay = xi_ref[chunk - 1:chunk, :]
        state_ref[...] = state * chunk_decay + lax.dot_general(
            k, v_decayed, _TN, preferred_element_type=F32)
        o = inner + cross
        mu = jnp.mean(o, axis=-1, keepdims=True)
        d = o - mu
        var = jnp.mean(d * d, axis=-1, keepdims=True)
        normed = d * lax.rsqrt(var + GN_EPS) * gain_ref[...]
        gate = g_ref[rows, :].astype(F32)
        o_ref[rows, :] = (gate * _sigmoid(gate) * normed).astype(o_ref.dtype)


def _retention_core(proj, gn_gain, batch, seq):
    m, width = proj.shape
    heads = RET_HEADS
    vwidth = gn_gain.shape[0]
    d_model = (width - 2 * vwidth) // 2
    dk = d_model // heads
    dv = vwidth // heads
    rows = RET_ROWS_PER_STEP
    chunk = RET_CHUNK
    steps = seq // rows
    assert seq % rows == 0 and rows % chunk == 0 and dk % (2 * V7X_LANES) == 0 and dv % V7X_LANES == 0

    half = dk // 2
    inv = ROPE_BASE ** (-jnp.arange(half, dtype=F32) / half)
    ang = jnp.arange(seq).astype(F32)[:, None] * inv[None, :]
    cos = jnp.concatenate([jnp.cos(ang), jnp.cos(ang)], axis=-1)
    sin_signed = jnp.concatenate([-jnp.sin(ang), jnp.sin(ang)], axis=-1)
    log_gamma = jnp.log1p(-jnp.power(2.0, -5.0 - jnp.arange(heads, dtype=F32)))

    k_off = d_model // dk
    v_off = 2 * d_model // dv
    g_off = (2 * d_model + vwidth) // dv
    row_map = lambda b, h, t: b * steps + t
    return pl.pallas_call(
        functools.partial(_retention_kernel, chunk=chunk, n_chunks=rows // chunk),
        grid=(batch, heads, steps),
        in_specs=[
            pl.BlockSpec(memory_space=pltpu.SMEM),
            pl.BlockSpec((rows, dk), lambda b, h, t: (row_map(b, h, t), h)),
            pl.BlockSpec((rows, dk), lambda b, h, t: (row_map(b, h, t), k_off + h)),
            pl.BlockSpec((rows, dv), lambda b, h, t: (row_map(b, h, t), v_off + h)),
            pl.BlockSpec((rows, dv), lambda b, h, t: (row_map(b, h, t), g_off + h)),
            pl.BlockSpec((rows, dk), lambda b, h, t: (t, 0)),
            pl.BlockSpec((rows, dk), lambda b, h, t: (t, 0)),
            pl.BlockSpec((1, dv), lambda b, h, t: (0, h)),
        ],
        out_specs=pl.BlockSpec((rows, dv), lambda b, h, t: (row_map(b, h, t), h)),
        out_shape=jax.ShapeDtypeStruct((m, vwidth), BF16),
        scratch_shapes=[pltpu.VMEM((dk, dv), F32),
                        pltpu.VMEM((chunk, chunk), F32),
                        pltpu.VMEM((chunk, dv), F32),
                        pltpu.VMEM((chunk, dv), F32)],
        compiler_params=_params("arbitrary", "arbitrary", "arbitrary"),
        name="retention_core",
    )(log_gamma, proj, proj, proj, proj, cos, sin_signed, gn_gain.reshape(1, vwidth).astype(F32))


def _t5_bucket_table(n_rel, n_buckets, max_dist):
    n = np.arange(n_rel)
    max_exact = n_buckets // 2
    nf = np.maximum(n, max_exact).astype(np.float64)
    large = max_exact + (np.log(nf / max_exact) / math.log(max_dist / max_exact)
                         * (n_buckets - max_exact)).astype(np.int64)
    large = np.minimum(large, n_buckets - 1)
    return np.where(n < max_exact, n, large).astype(np.int32)


def _moba_kernel(tbl_ref, q_ref, qall_ref, k_ref, vt_ref, bucket_ref, o_ref,
                 kmean_ref, bias_ref, far_bias_ref, pen_ref, s_ref, m_ref, l_ref, acc_ref,
                 *, group, n_blocks, blk, dh, topk, n_buckets, far_bucket, scale):
    hg = pl.program_id(1)
    qb = pl.program_id(2)
    heads = range(group)
    seq = n_blocks * blk
    slot_prev, slot_own = n_blocks - 2, n_blocks - 1
    to_log2 = math.log2(math.e)

    def cols(g):
        return slice(g * dh, (g + 1) * dh)

    @pl.when(qb == 0)
    def _start_of_heads():
        bucket_of_rel = bucket_ref[...]
        key_pos = lax.broadcasted_iota(jnp.int32, (blk, blk), 0)
        query_pos = lax.broadcasted_iota(jnp.int32, (blk, blk), 1)
        block_id = lax.broadcasted_iota(jnp.int32, (n_blocks, blk), 0)
        for g in heads:
            for j in range(n_blocks):
                kj = k_ref[pl.ds(j * blk, blk), cols(g)].astype(F32)
                kmean_ref[g, pl.ds(j, 1), :] = jnp.mean(kj, axis=0, keepdims=True)
            kmean = kmean_ref[g]
            piece0 = kmean.astype(BF16)
            rest = kmean - piece0.astype(F32)
            piece1 = rest.astype(BF16)
            piece2 = (rest - piece1.astype(F32)).astype(BF16)
            for i in range(n_blocks):
                qi = qall_ref[pl.ds(i * blk, blk), cols(g)]
                gate = (lax.dot_general(piece0, qi, _NT, preferred_element_type=F32)
                        + lax.dot_general(piece1, qi, _NT, preferred_element_type=F32)
                        + lax.dot_general(piece2, qi, _NT, preferred_element_type=F32))
                past = block_id < i
                gate = jnp.where(past, gate, NEG_INF)
                chosen = jnp.zeros((n_blocks, blk), jnp.bool_)
                for _ in range(topk):
                    best = jnp.max(gate, axis=0, keepdims=True)
                    first = jnp.min(jnp.where(gate == best, block_id, n_blocks), axis=0,
                                    keepdims=True)
                    pick = block_id == first
                    chosen = chosen | pick
                    gate = jnp.where(pick, -jnp.inf, gate)
                pen_ref[g, i] = jnp.where(chosen & past, 0.0, NEG_INF)

            bias_of_rel = jnp.zeros(bucket_of_rel.shape, F32)
            for b in range(n_buckets):
                bias_of_rel = jnp.where(bucket_of_rel == b, tbl_ref[hg * group + g, b], bias_of_rel)
            bias_of_rel = bias_of_rel * to_log2
            toeplitz = pltpu.roll(jnp.broadcast_to(bias_of_rel[0:1, :], (blk, 2 * blk)), 0, 1,
                                  stride=1, stride_axis=0)
            bias_ref[g, 0] = toeplitz[:, blk:]
            bias_ref[g, 1] = jnp.where(key_pos <= query_pos, toeplitz[:, :blk], NEG_INF)
            far_bias_ref[g] = jnp.full((1, blk), tbl_ref[hg * group + g, far_bucket], F32) * to_log2

    def qk(g, block):
        kj = k_ref[pl.ds(pl.multiple_of(block * blk, blk), blk), cols(g)]
        return lax.dot_general(kj, q_ref[:, cols(g)], _NT, preferred_element_type=F32)

    def col_max(s):
        return jnp.max(s, axis=0, keepdims=True)

    prev_block = jnp.maximum(qb - 1, 0)
    near_dots = [(qk(g, prev_block), qk(g, qb)) for g in heads]
    for g in heads:
        d_prev, d_own = near_dots[g]
        s_prev = d_prev * (scale * to_log2) + bias_ref[g, 0] + pen_ref[g, qb, pl.ds(prev_block, 1), :]
        s_own = d_own * (scale * to_log2) + bias_ref[g, 1]
        s_ref[g, slot_prev] = s_prev
        s_ref[g, slot_own] = s_own
        m_ref[g] = jnp.maximum(col_max(s_prev), col_max(s_own))

    n_far = jnp.maximum(qb - 1, 0)
    n_far_trips = (n_far + 1) // 2

    def far_pair(i):
        return [(2 * i + e, jnp.minimum(2 * i + e, n_far - 1), 2 * i + e < n_far)
                for e in range(2)]

    def far_pass_a(i, carry):
        blocks = far_pair(i)
        dots = [[qk(g, block) for _, block, _ in blocks] for g in heads]
        for g in heads:
            m = m_ref[g]
            for (slot, block, real), d in zip(blocks, dots[g]):
                penalty = jnp.where(real, pen_ref[g, qb, pl.ds(block, 1), :], NEG_INF)
                s = d * (scale * to_log2) + (penalty + far_bias_ref[g])
                s_ref[g, slot] = s
                m = jnp.maximum(m, col_max(s))
            m_ref[g] = m
        return carry

    lax.fori_loop(0, n_far_trips, far_pass_a, 0)

    def probabilities(g, slots):
        m = m_ref[g]
        ps = [jnp.exp2(s_ref[g, slot] - m) for slot in slots]
        total = ps[0].sum(axis=0, keepdims=True)
        for p in ps[1:]:
            total = total + p.sum(axis=0, keepdims=True)
        return total, [p.astype(BF16) for p in ps]

    near = [probabilities(g, (slot_prev, slot_own)) for g in heads]
    for g in heads:
        total, (p_prev, p_own) = near[g]
        l_ref[g] = total
        acc_ref[g] = (jnp.dot(vt_ref[g, prev_block], p_prev, preferred_element_type=F32)
                      + jnp.dot(vt_ref[g, qb], p_own, preferred_element_type=F32))

    def far_pass_b(i, carry):
        blocks = far_pair(i)
        far = [probabilities(g, [slot for slot, _, _ in blocks]) for g in heads]
        for g in heads:
            total, ps = far[g]
            l_ref[g] = l_ref[g] + total
            acc = acc_ref[g]
            for (_, block, _), p in zip(blocks, ps):
                acc = acc + jnp.dot(vt_ref[g, block], p, preferred_element_type=F32)
            acc_ref[g] = acc
        return carry

    lax.fori_loop(0, n_far_trips, far_pass_b, 0)

    for g in heads:
        o_ref[:, cols(g)] = (acc_ref[g] / l_ref[g]).T.astype(o_ref.dtype)


def _moba_core(qkv, rel_bias, batch, seq):
    m, width = qkv.shape
    d_model = width // 3
    n_buckets, heads = rel_bias.shape
    dh = d_model // heads
    blk = MOBA_BLOCK
    group = MOBA_HEADS_PER_STEP
    n_blocks = seq // blk
    assert seq % blk == 0 and dh % V7X_LANES == 0 and heads % group == 0
    assert blk & (blk - 1) == 0 and n_blocks >= 2

    buckets = _t5_bucket_table(max(seq, 2 * blk), n_buckets, REL_MAX_DIST)
    far_bucket = int(buckets[blk + 1])
    assert np.all(buckets[blk + 1:] == far_bucket), "blocks two or more back must share one bucket"
    bucket_of_rel = np.broadcast_to(buckets[None, :2 * blk], (V7X_SUBLANES, 2 * blk))

    v = qkv[:, 2 * d_model:].reshape(batch, n_blocks, blk, heads, dh)
    vt = jnp.transpose(v, (0, 3, 1, 4, 2))

    gw = group * dh
    n_groups = heads // group
    return pl.pallas_call(
        functools.partial(_moba_kernel, group=group, n_blocks=n_blocks, blk=blk, dh=dh,
                          topk=min(MOBA_TOPK, n_blocks), n_buckets=n_buckets,
                          far_bucket=far_bucket, scale=dh ** -0.5),
        grid=(batch, n_groups, n_blocks),
        in_specs=[
            pl.BlockSpec(memory_space=pltpu.SMEM),
            pl.BlockSpec((blk, gw), lambda b, h, i: (b * n_blocks + i, h)),
            pl.BlockSpec((seq, gw), lambda b, h, i: (b, h)),
            pl.BlockSpec((seq, gw), lambda b, h, i: (b, n_groups + h)),
            pl.BlockSpec((None, group, n_blocks, dh, blk), lambda b, h, i: (b, h, 0, 0, 0)),
            pl.BlockSpec((V7X_SUBLANES, 2 * blk), lambda b, h, i: (0, 0)),
        ],
        out_specs=pl.BlockSpec((blk, gw), lambda b, h, i: (b * n_blocks + i, h)),
        out_shape=jax.ShapeDtypeStruct((m, d_model), BF16),
        scratch_shapes=[pltpu.VMEM((group, n_blocks, dh), F32),
                        pltpu.VMEM((group, 2, blk, blk), F32),
                        pltpu.VMEM((group, 1, blk), F32),
                        pltpu.VMEM((group, n_blocks, n_blocks, blk), F32),
                        pltpu.VMEM((group, n_blocks, blk, blk), F32),
                        pltpu.VMEM((group, 1, blk), F32),
                        pltpu.VMEM((group, 1, blk), F32),
                        pltpu.VMEM((group, dh, blk), F32)],
        compiler_params=_params("arbitrary", "arbitrary", "arbitrary"),
        name="moba_core",
    )(rel_bias.T.astype(F32), qkv, qkv, qkv, vt, jnp.asarray(bucket_of_rel))


def _ffn_up_kernel(x_ref, wg_ref, wv_ref, cwg_ref, cwv_ref, cbg_ref, cbv_ref, o_ref,
                   wgb_ref, wvb_ref, carry_ref, *, tiles_per_seq):
    _cast_weight_tile(wg_ref, wgb_ref)
    _cast_weight_tile(wv_ref, wvb_ref)

    @pl.when(pl.program_id(1) % tiles_per_seq == 0)
    def _start_of_sequence():
        carry_ref[...] = jnp.zeros_like(carry_ref)

    x = x_ref[...]
    tm = x.shape[0]
    tn = o_ref.shape[1]
    row = lax.broadcasted_iota(jnp.int32, (tm, tn), 0)

    def conv_branch(w_ref, cw_ref, cb_ref, slot):
        u = jnp.dot(x, w_ref[...], preferred_element_type=F32)
        tail = carry_ref[slot]
        prev1 = tail[V7X_SUBLANES - 1:V7X_SUBLANES, :]
        prev2 = tail[V7X_SUBLANES - 2:V7X_SUBLANES - 1, :]
        back1 = jnp.where(row == 0, prev1, pltpu.roll(u, 1, axis=0))
        back2 = jnp.where(row == 0, prev2, jnp.where(row == 1, prev1, pltpu.roll(u, 2, axis=0)))
        carry_ref[slot] = u[tm - V7X_SUBLANES:, :]
        cw = cw_ref[...]
        return cw[2:3, :] * u + cw[1:2, :] * back1 + cw[0:1, :] * back2 + cb_ref[...]

    gate = conv_branch(wgb_ref, cwg_ref, cbg_ref, 0)
    val = conv_branch(wvb_ref, cwv_ref, cbv_ref, 1)
    o_ref[...] = (gate * _sigmoid(gate) * val).astype(o_ref.dtype)


def _ffn_up(x, w_up, conv_w, conv_b, layer, seq, *, tm, tn):
    m, k = x.shape
    d_ff = w_up.shape[2] // 2
    n_col = d_ff // tn
    assert conv_w.shape[1] == CONV_WIDTH == 3 and seq % tm == 0 and d_ff % tn == 0
    return pl.pallas_call(
        functools.partial(_ffn_up_kernel, tiles_per_seq=seq // tm),
        grid=(n_col, m // tm),
        in_specs=[
            pl.BlockSpec((tm, k), lambda j, i: (i, 0)),
            pl.BlockSpec((None, k, tn), lambda j, i: (layer, 0, j)),
            pl.BlockSpec((None, k, tn), lambda j, i: (layer, 0, n_col + j)),
            pl.BlockSpec((None, CONV_WIDTH, tn), lambda j, i: (layer, 0, j)),
            pl.BlockSpec((None, CONV_WIDTH, tn), lambda j, i: (layer, 0, n_col + j)),
            pl.BlockSpec((None, 1, tn), lambda j, i: (layer, 0, j)),
            pl.BlockSpec((None, 1, tn), lambda j, i: (layer, 0, n_col + j)),
        ],
        out_specs=pl.BlockSpec((tm, tn), lambda j, i: (i, j)),
        out_shape=jax.ShapeDtypeStruct((m, d_ff), BF16),
        scratch_shapes=[pltpu.VMEM((k, tn), BF16),
                        pltpu.VMEM((k, tn), BF16),
                        pltpu.VMEM((2, V7X_SUBLANES, tn), F32)],
        compiler_params=_params("arbitrary", "arbitrary"),
        name="ffn_up_conv_gate",
    )(x, w_up, w_up, conv_w, conv_w, conv_b, conv_b)


def kernel(x, mix_norm, ret_w_in, ret_gn, ret_w_out, moba_w_qkv, moba_w_out, rel_bias,
           ffn_norm, ffn_w_up, ffn_conv_w, ffn_conv_b, ffn_w_down, final_norm):
    batch, seq, d_model = x.shape
    depth = mix_norm.shape[0]
    conv_w = ffn_conv_w.astype(F32)
    conv_b = ffn_conv_b.astype(F32)[:, None, :]
    h = x.reshape(batch * seq, d_model)
    for i in range(depth):
        hn = _rmsnorm(h, mix_norm[i], BF16)
        j = i // N_MIXERS
        if i % N_MIXERS == 0:
            proj = _matmul(hn, ret_w_in, j, BF16, tm=1024, tn=1024, name="ret_in_proj")
            y = _retention_core(proj, ret_gn[j], batch, seq)
            h = _matmul(y, ret_w_out, j, F32, h, tm=1024, tn=512, name="ret_out_proj")
        else:
            qkv = _matmul(hn, moba_w_qkv, j, BF16, tm=1024, tn=1024, name="moba_qkv_proj")
            o = _moba_core(qkv, rel_bias, batch, seq)
            h = _matmul(o, moba_w_out, j, F32, h, tm=1024, tn=1024, name="moba_out_proj")
        hn = _rmsnorm(h, ffn_norm[i], BF16)
        a = _ffn_up(hn, ffn_w_up, conv_w, conv_b, i, seq, tm=1024, tn=512)
        h = _matmul(a, ffn_w_down, i, F32, h, tm=512, tn=512, name="ffn_down_proj")
    return _rmsnorm(h, final_norm, F32).reshape(batch, seq, d_model)
```

```python
import functools
import math

import numpy as np
import jax
import jax.numpy as jnp
from jax import lax
from jax.experimental import pallas as pl
from jax.experimental.pallas import tpu as pltpu

F32 = jnp.float32
BF16 = jnp.bfloat16

N_MIXERS = 2
RET_HEADS = 8
ROPE_BASE = 10000.0
MOBA_BLOCK = 256
MOBA_TOPK = 3
REL_MAX_DIST = 128
CONV_WIDTH = 3
RMS_EPS = 1e-6
GN_EPS = 1e-5
NEG_INF = -1e30

RET_CHUNK = 256
RET_ROWS_PER_STEP = 512
MOBA_HEADS_PER_STEP = 4

V7X_SUBLANES = 8
V7X_LANES = 128
V7X_VMEM_LIMIT_BYTES = 56 * 1024 * 1024

_NT = (((1,), (1,)), ((), ()))
_TN = (((0,), (0,)), ((), ()))


def _params(*semantics):
    return pltpu.CompilerParams(dimension_semantics=semantics,
                                vmem_limit_bytes=V7X_VMEM_LIMIT_BYTES)


def _sigmoid(x):
    return 1.0 / (1.0 + jnp.exp(-x))


def _rmsnorm_kernel(x_ref, g_ref, o_ref):
    x = x_ref[...]
    ms = jnp.mean(x * x, axis=-1, keepdims=True)
    o_ref[...] = (x * lax.rsqrt(ms + RMS_EPS) * g_ref[...]).astype(o_ref.dtype)


def _rmsnorm(x, g, out_dtype, tm=512):
    m, d = x.shape
    return pl.pallas_call(
        _rmsnorm_kernel,
        grid=(m // tm,),
        in_specs=[pl.BlockSpec((tm, d), lambda i: (i, 0)),
                  pl.BlockSpec((1, d), lambda i: (0, 0))],
        out_specs=pl.BlockSpec((tm, d), lambda i: (i, 0)),
        out_shape=jax.ShapeDtypeStruct((m, d), out_dtype),
        compiler_params=_params("arbitrary"),
        name="rmsnorm",
    )(x, g.reshape(1, d).astype(F32))


def _cast_weight_tile(w_ref, wb_ref):
    @pl.when(pl.program_id(1) == 0)
    def _():
        wb_ref[...] = w_ref[...].astype(BF16)


def _matmul_kernel(a_ref, w_ref, o_ref, wb_ref):
    _cast_weight_tile(w_ref, wb_ref)
    o_ref[...] = jnp.dot(a_ref[...], wb_ref[...],
                         preferred_element_type=F32).astype(o_ref.dtype)


def _matmul_residual_kernel(a_ref, w_ref, r_ref, o_ref, wb_ref):
    _cast_weight_tile(w_ref, wb_ref)
    o_ref[...] = (r_ref[...] + jnp.dot(a_ref[...], wb_ref[...],
                                       preferred_element_type=F32)).astype(o_ref.dtype)


def _matmul_silu_kernel(a_ref, w_ref, o_ref, wb_ref):
    _cast_weight_tile(w_ref, wb_ref)
    x = jnp.dot(a_ref[...], wb_ref[...], preferred_element_type=F32)
    o_ref[...] = (x * _sigmoid(x)).astype(o_ref.dtype)


def _matmul_rope_kernel(a_ref, w_ref, cos_ref, sin_ref, o_ref, wb_ref, *, head_dim, key_tile0,
                        key_scale):
    _cast_weight_tile(w_ref, wb_ref)
    a = a_ref[...]
    half = head_dim // 2
    scale = jnp.where(pl.program_id(0) >= key_tile0, key_scale, 1.0)
    for h in range(o_ref.shape[1] // head_dim):
        cols = slice(h * head_dim, (h + 1) * head_dim)
        x = jnp.dot(a, wb_ref[:, cols], preferred_element_type=F32)
        swapped = jnp.concatenate([x[:, half:], x[:, :half]], axis=1)
        rotated = x * cos_ref[...] + swapped * sin_ref[...]
        o_ref[:, cols] = (rotated * scale).astype(o_ref.dtype)


def _matmul(a, w, layer, out_dtype, residual=None, *, tm, tn, name, col_start=0, n_cols=None,
            silu=False, rope=None):
    m, k = a.shape
    n = w.shape[2] - col_start if n_cols is None else n_cols
    tile0 = col_start // tn
    assert col_start % tn == 0 and n % tn == 0 and m % tm == 0
    in_specs = [pl.BlockSpec((tm, k), lambda j, i: (i, 0)),
                pl.BlockSpec((None, k, tn), lambda j, i: (layer, 0, tile0 + j))]
    args = [a, w]
    body = _matmul_kernel
    if residual is not None:
        in_specs.append(pl.BlockSpec((tm, tn), lambda j, i: (i, j)))
        args.append(residual)
        body = _matmul_residual_kernel
    elif silu:
        body = _matmul_silu_kernel
    elif rope is not None:
        head_dim = rope["cos"].shape[1]
        tiles_per_seq = rope["seq"] // tm
        assert rope["seq"] % tm == 0 and tn % head_dim == 0 and rope["key_col"] % tn == 0
        table_spec = pl.BlockSpec((tm, head_dim), lambda j, i: (i % tiles_per_seq, 0))
        in_specs += [table_spec, table_spec]
        args += [rope["cos"], rope["sin_signed"]]
        body = functools.partial(_matmul_rope_kernel, head_dim=head_dim,
                                 key_tile0=(rope["key_col"] - col_start) // tn,
                                 key_scale=rope["key_scale"])
    return pl.pallas_call(
        body,
        grid=(n // tn, m // tm),
        in_specs=in_specs,
        out_specs=pl.BlockSpec((tm, tn), lambda j, i: (i, j)),
        out_shape=jax.ShapeDtypeStruct((m, n), out_dtype),
        scratch_shapes=[pltpu.VMEM((k, tn), BF16)],
        compiler_params=_params("arbitrary", "arbitrary"),
        name=name,
    )(*args)


def _retention_kernel(lg_ref, q_ref, k_ref, v_ref, g_ref, gain_ref, o_ref,
                      state_ref, decay_ref, xi_ref, zeta_ref, *, chunk, n_chunks):
    h = pl.program_id(1)
    dv = v_ref.shape[1]
    log_gamma = lg_ref[h]

    @pl.when(pl.program_id(2) == 0)
    def _start_of_sequence():
        state_ref[...] = jnp.zeros_like(state_ref)
        r = lax.broadcasted_iota(jnp.int32, (chunk, chunk), 0)
        c = lax.broadcasted_iota(jnp.int32, (chunk, chunk), 1)
        diff = (r - c).astype(F32)
        decay_ref[...] = jnp.where(diff >= 0, jnp.exp(log_gamma * jnp.maximum(diff, 0.0)), 0.0)
        idx = lax.broadcasted_iota(jnp.int32, (chunk, dv), 0).astype(F32)
        xi_ref[...] = jnp.exp(log_gamma * (idx + 1.0))
        zeta_ref[...] = jnp.exp(log_gamma * (chunk - 1.0 - idx))

    for ci in range(n_chunks):
        rows = pl.ds(ci * chunk, chunk)
        q = q_ref[rows, :]
        k = k_ref[rows, :]
        v = v_ref[rows, :]
        scores = lax.dot_general(q, k, _NT, preferred_element_type=F32) * decay_ref[...]
        inner = jnp.dot(scores.astype(BF16), v, preferred_element_type=F32)
        state = state_ref[...]
        cross = jnp.dot(q, state.astype(BF16), preferred_element_type=F32) * xi_ref[...]
        v_decayed = (v.astype(F32) * zeta_ref[...]).astype(BF16)
        chunk_decay = xi_ref[chunk - 1:chunk, :]
        state_ref[...] = state * chunk_decay + lax.dot_general(
            k, v_decayed, _TN, preferred_element_type=F32)
        o = inner + cross
        mu = jnp.mean(o, axis=-1, keepdims=True)
        d = o - mu
        var = jnp.mean(d * d, axis=-1, keepdims=True)
        normed = d * lax.rsqrt(var + GN_EPS) * gain_ref[...]
        o_ref[rows, :] = (g_ref[rows, :].astype(F32) * normed).astype(o_ref.dtype)


def _retention_core(qk, v, gate, gn_gain, batch, seq):
    m, vwidth = v.shape
    heads = RET_HEADS
    dk = qk.shape[1] // (2 * heads)
    dv = vwidth // heads
    rows = RET_ROWS_PER_STEP
    chunk = RET_CHUNK
    steps = seq // rows
    assert seq % rows == 0 and rows % chunk == 0 and dk % V7X_LANES == 0 and dv % V7X_LANES == 0
    log_gamma = jnp.log1p(-jnp.power(2.0, -5.0 - jnp.arange(heads, dtype=F32)))

    row_map = lambda b, h, t: b * steps + t
    return pl.pallas_call(
        functools.partial(_retention_kernel, chunk=chunk, n_chunks=rows // chunk),
        grid=(batch, heads, steps),
        in_specs=[
            pl.BlockSpec(memory_space=pltpu.SMEM),
            pl.BlockSpec((rows, dk), lambda b, h, t: (row_map(b, h, t), h)),
            pl.BlockSpec((rows, dk), lambda b, h, t: (row_map(b, h, t), heads + h)),
            pl.BlockSpec((rows, dv), lambda b, h, t: (row_map(b, h, t), h)),
            pl.BlockSpec((rows, dv), lambda b, h, t: (row_map(b, h, t), h)),
            pl.BlockSpec((1, dv), lambda b, h, t: (0, h)),
        ],
        out_specs=pl.BlockSpec((rows, dv), lambda b, h, t: (row_map(b, h, t), h)),
        out_shape=jax.ShapeDtypeStruct((m, vwidth), BF16),
        scratch_shapes=[pltpu.VMEM((dk, dv), F32),
                        pltpu.VMEM((chunk, chunk), F32),
                        pltpu.VMEM((chunk, dv), F32),
                        pltpu.VMEM((chunk, dv), F32)],
        compiler_params=_params("arbitrary", "arbitrary", "arbitrary"),
        name="retention_core",
    )(log_gamma, qk, qk, v, gate, gn_gain.reshape(1, vwidth).astype(F32))


def _rope_tables(seq, head_dim):
    half = head_dim // 2
    inv = ROPE_BASE ** (-jnp.arange(half, dtype=F32) / half)
    ang = jnp.arange(seq).astype(F32)[:, None] * inv[None, :]
    cos = jnp.concatenate([jnp.cos(ang), jnp.cos(ang)], axis=-1)
    sin_signed = jnp.concatenate([-jnp.sin(ang), jnp.sin(ang)], axis=-1)
    return cos, sin_signed


def _t5_bucket_table(n_rel, n_buckets, max_dist):
    n = np.arange(n_rel)
    max_exact = n_buckets // 2
    nf = np.maximum(n, max_exact).astype(np.float64)
    large = max_exact + (np.log(nf / max_exact) / math.log(max_dist / max_exact)
                         * (n_buckets - max_exact)).astype(np.int64)
    large = np.minimum(large, n_buckets - 1)
    return np.where(n < max_exact, n, large).astype(np.int32)


def _moba_kernel(tbl_ref, q_ref, qall_ref, k_ref, vt_ref, bucket_ref, o_ref,
                 kmean_ref, bias_ref, far_bias_ref, pen_ref, s_ref, m_ref, l_ref, acc_ref,
                 *, group, n_blocks, blk, dh, topk, n_buckets, far_bucket, scale):
    hg = pl.program_id(1)
    qb = pl.program_id(2)
    heads = range(group)
    seq = n_blocks * blk
    slot_prev, slot_own = n_blocks - 2, n_blocks - 1
    to_log2 = math.log2(math.e)

    def cols(g):
        return slice(g * dh, (g + 1) * dh)

    @pl.when(qb == 0)
    def _start_of_heads():
        bucket_of_rel = bucket_ref[...]
        key_pos = lax.broadcasted_iota(jnp.int32, (blk, blk), 0)
        query_pos = lax.broadcasted_iota(jnp.int32, (blk, blk), 1)
        block_id = lax.broadcasted_iota(jnp.int32, (n_blocks, blk), 0)
        for g in heads:
            for j in range(n_blocks):
                kj = k_ref[pl.ds(j * blk, blk), cols(g)].astype(F32)
                kmean_ref[g, pl.ds(j, 1), :] = jnp.mean(kj, axis=0, keepdims=True)
            kmean = kmean_ref[g]
            piece0 = kmean.astype(BF16)
            rest = kmean - piece0.astype(F32)
            piece1 = rest.astype(BF16)
            piece2 = (rest - piece1.astype(F32)).astype(BF16)
            for i in range(n_blocks):
                qi = qall_ref[pl.ds(i * blk, blk), cols(g)]
                gate = (lax.dot_general(piece0, qi, _NT, preferred_element_type=F32)
                        + lax.dot_general(piece1, qi, _NT, preferred_element_type=F32)
                        + lax.dot_general(piece2, qi, _NT, preferred_element_type=F32))
                past = block_id < i
                gate = jnp.where(past, gate, NEG_INF)
                chosen = jnp.zeros((n_blocks, blk), jnp.bool_)
                for _ in range(topk):
                    best = jnp.max(gate, axis=0, keepdims=True)
                    first = jnp.min(jnp.where(gate == best, block_id, n_blocks), axis=0,
                                    keepdims=True)
                    pick = block_id == first
                    chosen = chosen | pick
                    gate = jnp.where(pick, -jnp.inf, gate)
                pen_ref[g, i] = jnp.where(chosen & past, 0.0, NEG_INF)

            bias_of_rel = jnp.zeros(bucket_of_rel.shape, F32)
            for b in range(n_buckets):
                bias_of_rel = jnp.where(bucket_of_rel == b, tbl_ref[hg * group + g, b], bias_of_rel)
            bias_of_rel = bias_of_rel * to_log2
            toeplitz = pltpu.roll(jnp.broadcast_to(bias_of_rel[0:1, :], (blk, 2 * blk)), 0, 1,
                                  stride=1, stride_axis=0)
            bias_ref[g, 0] = toeplitz[:, blk:]
            bias_ref[g, 1] = jnp.where(key_pos <= query_pos, toeplitz[:, :blk], NEG_INF)
            far_bias_ref[g] = jnp.full((1, blk), tbl_ref[hg * group + g, far_bucket], F32) * to_log2

    def qk(g, block):
        kj = k_ref[pl.ds(pl.multiple_of(block * blk, blk), blk), cols(g)]
        return lax.dot_general(kj, q_ref[:, cols(g)], _NT, preferred_element_type=F32)

    def col_max(s):
        return jnp.max(s, axis=0, keepdims=True)

    prev_block = jnp.maximum(qb - 1, 0)
    near_dots = [(qk(g, prev_block), qk(g, qb)) for g in heads]
    for g in heads:
        d_prev, d_own = near_dots[g]
        s_prev = d_prev * (scale * to_log2) + bias_ref[g, 0] + pen_ref[g, qb, pl.ds(prev_block, 1), :]
        s_own = d_own * (scale * to_log2) + bias_ref[g, 1]
        s_ref[g, slot_prev] = s_prev
        s_ref[g, slot_own] = s_own
        m_ref[g] = jnp.maximum(col_max(s_prev), col_max(s_own))

    n_far = jnp.maximum(qb - 1, 0)
    n_far_trips = (n_far + 1) // 2

    def far_pair(i):
        return [(2 * i + e, jnp.minimum(2 * i + e, n_far - 1), 2 * i + e < n_far)
                for e in range(2)]

    def far_pass_a(i, carry):
        blocks = far_pair(i)
        dots = [[qk(g, block) for _, block, _ in blocks] for g in heads]
        for g in heads:
            m = m_ref[g]
            for (slot, block, real), d in zip(blocks, dots[g]):
                penalty = jnp.where(real, pen_ref[g, qb, pl.ds(block, 1), :], NEG_INF)
                s = d * (scale * to_log2) + (penalty + far_bias_ref[g])
                s_ref[g, slot] = s
                m = jnp.maximum(m, col_max(s))
            m_ref[g] = m
        return carry

    lax.fori_loop(0, n_far_trips, far_pass_a, 0)

    def probabilities(g, slots):
        m = m_ref[g]
        ps = [jnp.exp2(s_ref[g, slot] - m) for slot in slots]
        total = ps[0].sum(axis=0, keepdims=True)
        for p in ps[1:]:
            total = total + p.sum(axis=0, keepdims=True)
        return total, [p.astype(BF16) for p in ps]

    near = [probabilities(g, (slot_prev, slot_own)) for g in heads]
    for g in heads:
        total, (p_prev, p_own) = near[g]
        l_ref[g] = total
        acc_ref[g] = (jnp.dot(vt_ref[g, prev_block], p_prev, preferred_element_type=F32)
                      + jnp.dot(vt_ref[g, qb], p_own, preferred_element_type=F32))

    def far_pass_b(i, carry):
        blocks = far_pair(i)
        far = [probabilities(g, [slot for slot, _, _ in blocks]) for g in heads]
        for g in heads:
            total, ps = far[g]
            l_ref[g] = l_ref[g] + total
            acc = acc_ref[g]
            for (_, block, _), p in zip(blocks, ps):
                acc = acc + jnp.dot(vt_ref[g, block], p, preferred_element_type=F32)
            acc_ref[g] = acc
        return carry

    lax.fori_loop(0, n_far_trips, far_pass_b, 0)

    for g in heads:
        o_ref[:, cols(g)] = (acc_ref[g] / l_ref[g]).T.astype(o_ref.dtype)


def _moba_core(qkv, rel_bias, batch, seq):
    m, width = qkv.shape
    d_model = width // 3
    n_buckets, heads = rel_bias.shape
    dh = d_model // heads
    blk = MOBA_BLOCK
    group = MOBA_HEADS_PER_STEP
    n_blocks = seq // blk
    assert seq % blk == 0 and dh % V7X_LANES == 0 and heads % group == 0
    assert blk & (blk - 1) == 0 and n_blocks >= 2

    buckets = _t5_bucket_table(max(seq, 2 * blk), n_buckets, REL_MAX_DIST)
    far_bucket = int(buckets[blk + 1])
    assert np.all(buckets[blk + 1:] == far_bucket), "blocks two or more back must share one bucket"
    bucket_of_rel = np.broadcast_to(buckets[None, :2 * blk], (V7X_SUBLANES, 2 * blk))

    v = qkv[:, 2 * d_model:].reshape(batch, n_blocks, blk, heads, dh)
    vt = jnp.transpose(v, (0, 3, 1, 4, 2))

    gw = group * dh
    n_groups = heads // group
    return pl.pallas_call(
        functools.partial(_moba_kernel, group=group, n_blocks=n_blocks, blk=blk, dh=dh,
                          topk=min(MOBA_TOPK, n_blocks), n_buckets=n_buckets,
                          far_bucket=far_bucket, scale=dh ** -0.5),
        grid=(batch, n_groups, n_blocks),
        in_specs=[
            pl.BlockSpec(memory_space=pltpu.SMEM),
            pl.BlockSpec((blk, gw), lambda b, h, i: (b * n_blocks + i, h)),
            pl.BlockSpec((seq, gw), lambda b, h, i: (b, h)),
            pl.BlockSpec((seq, gw), lambda b, h, i: (b, n_groups + h)),
            pl.BlockSpec((None, group, n_blocks, dh, blk), lambda b, h, i: (b, h, 0, 0, 0)),
            pl.BlockSpec((V7X_SUBLANES, 2 * blk), lambda b, h, i: (0, 0)),
        ],
        out_specs=pl.BlockSpec((blk, gw), lambda b, h, i: (b * n_blocks + i, h)),
        out_shape=jax.ShapeDtypeStruct((m, d_model), BF16),
        scratch_shapes=[pltpu.VMEM((group, n_blocks, dh), F32),
                        pltpu.VMEM((group, 2, blk, blk), F32),
                        pltpu.VMEM((group, 1, blk), F32),
                        pltpu.VMEM((group, n_blocks, n_blocks, blk), F32),
                        pltpu.VMEM((group, n_blocks, blk, blk), F32),
                        pltpu.VMEM((group, 1, blk), F32),
                        pltpu.VMEM((group, 1, blk), F32),
                        pltpu.VMEM((group, dh, blk), F32)],
        compiler_params=_params("arbitrary", "arbitrary", "arbitrary"),
        name="moba_core",
    )(rel_bias.T.astype(F32), qkv, qkv, qkv, vt, jnp.asarray(bucket_of_rel))


def _ffn_up_kernel(x_ref, wg_ref, wv_ref, cwg_ref, cwv_ref, cbg_ref, cbv_ref, o_ref,
                   wgb_ref, wvb_ref, carry_ref, *, tiles_per_seq):
    _cast_weight_tile(wg_ref, wgb_ref)
    _cast_weight_tile(wv_ref, wvb_ref)

    @pl.when(pl.program_id(1) % tiles_per_seq == 0)
    def _start_of_sequence():
        carry_ref[...] = jnp.zeros_like(carry_ref)

    x = x_ref[...]
    tm = x.shape[0]
    tn = o_ref.shape[1]
    row = lax.broadcasted_iota(jnp.int32, (tm, tn), 0)

    def conv_branch(w_ref, cw_ref, cb_ref, slot):
        u = jnp.dot(x, w_ref[...], preferred_element_type=F32)
        tail = carry_ref[slot]
        prev1 = tail[V7X_SUBLANES - 1:V7X_SUBLANES, :]
        prev2 = tail[V7X_SUBLANES - 2:V7X_SUBLANES - 1, :]
        back1 = jnp.where(row == 0, prev1, pltpu.roll(u, 1, axis=0))
        back2 = jnp.where(row == 0, prev2, jnp.where(row == 1, prev1, pltpu.roll(u, 2, axis=0)))
        carry_ref[slot] = u[tm - V7X_SUBLANES:, :]
        cw = cw_ref[...]
        return cw[2:3, :] * u + cw[1:2, :] * back1 + cw[0:1, :] * back2 + cb_ref[...]

    gate = conv_branch(wgb_ref, cwg_ref, cbg_ref, 0)
    val = conv_branch(wvb_ref, cwv_ref, cbv_ref, 1)
    o_ref[...] = (gate * _sigmoid(gate) * val).astype(o_ref.dtype)


def _ffn_up(x, w_up, conv_w, conv_b, layer, seq, *, tm, tn):
    m, k = x.shape
    d_ff = w_up.shape[2] // 2
    n_col = d_ff // tn
    assert conv_w.shape[1] == CONV_WIDTH == 3 and seq % tm == 0 and d_ff % tn == 0
    return pl.pallas_call(
        functools.partial(_ffn_up_kernel, tiles_per_seq=seq // tm),
        grid=(n_col, m // tm),
        in_specs=[
            pl.BlockSpec((tm, k), lambda j, i: (i, 0)),
            pl.BlockSpec((None, k, tn), lambda j, i: (layer, 0, j)),
            pl.BlockSpec((None, k, tn), lambda j, i: (layer, 0, n_col + j)),
            pl.BlockSpec((None, CONV_WIDTH, tn), lambda j, i: (layer, 0, j)),
            pl.BlockSpec((None, CONV_WIDTH, tn), lambda j, i: (layer, 0, n_col + j)),
            pl.BlockSpec((None, 1, tn), lambda j, i: (layer, 0, j)),
            pl.BlockSpec((None, 1, tn), lambda j, i: (layer, 0, n_col + j)),
        ],
        out_specs=pl.BlockSpec((tm, tn), lambda j, i: (i, j)),
        out_shape=jax.ShapeDtypeStruct((m, d_ff), BF16),
        scratch_shapes=[pltpu.VMEM((k, tn), BF16),
                        pltpu.VMEM((k, tn), BF16),
                        pltpu.VMEM((2, V7X_SUBLANES, tn), F32)],
        compiler_params=_params("arbitrary", "arbitrary"),
        name="ffn_up_conv_gate",
    )(x, w_up, w_up, conv_w, conv_w, conv_b, conv_b)


def kernel(x, mix_norm, ret_w_in, ret_gn, ret_w_out, moba_w_qkv, moba_w_out, rel_bias,
           ffn_norm, ffn_w_up, ffn_conv_w, ffn_conv_b, ffn_w_down, final_norm):
    batch, seq, d_model = x.shape
    depth = mix_norm.shape[0]
    conv_w = ffn_conv_w.astype(F32)
    conv_b = ffn_conv_b.astype(F32)[:, None, :]
    h = x.reshape(batch * seq, d_model)
    for i in range(depth):
        hn = _rmsnorm(h, mix_norm[i], BF16)
        j = i // N_MIXERS
        if i % N_MIXERS == 0:
            vwidth = ret_gn.shape[1]
            dk = d_model // RET_HEADS
            cos, sin_signed = _rope_tables(seq, dk)
            rope = dict(cos=cos, sin_signed=sin_signed, seq=seq, key_col=d_model,
                        key_scale=dk ** -0.5)
            qk = _matmul(hn, ret_w_in, j, BF16, tm=1024, tn=1024, name="ret_qk_proj",
                         n_cols=2 * d_model, rope=rope)
            v = _matmul(hn, ret_w_in, j, BF16, tm=1024, tn=1024, name="ret_v_proj",
                        col_start=2 * d_model, n_cols=vwidth)
            gate = _matmul(hn, ret_w_in, j, BF16, tm=1024, tn=1024, name="ret_gate_proj",
                           col_start=2 * d_model + vwidth, n_cols=vwidth, silu=True)
            y = _retention_core(qk, v, gate, ret_gn[j], batch, seq)
            h = _matmul(y, ret_w_out, j, F32, h, tm=1024, tn=512, name="ret_out_proj")
        else:
            qkv = _matmul(hn, moba_w_qkv, j, BF16, tm=1024, tn=1024, name="moba_qkv_proj")
            o = _moba_core(qkv, rel_bias, batch, seq)
            h = _matmul(o, moba_w_out, j, F32, h, tm=1024, tn=1024, name="moba_out_proj")
        hn = _rmsnorm(h, ffn_norm[i], BF16)
        a = _ffn_up(hn, ffn_w_up, conv_w, conv_b, i, seq, tm=1024, tn=512)
        h = _matmul(a, ffn_w_down, i, F32, h, tm=512, tn=512, name="ffn_down_proj")
    return _rmsnorm(h, final_norm, F32).reshape(batch, seq, d_model)
```

```python
import functools
import math

import numpy as np
import jax
import jax.numpy as jnp
from jax import lax
from jax.experimental import pallas as pl
from jax.experimental.pallas import tpu as pltpu

F32 = jnp.float32
BF16 = jnp.bfloat16

N_MIXERS = 2
RET_HEADS = 8
ROPE_BASE = 10000.0
MOBA_BLOCK = 256
MOBA_TOPK = 3
REL_MAX_DIST = 128
CONV_WIDTH = 3
RMS_EPS = 1e-6
GN_EPS = 1e-5
NEG_INF = -1e30

RET_CHUNK = 256
RET_ROWS_PER_STEP = 512
MOBA_HEADS_PER_STEP = 4
MOBA_BLOCKS_PER_WIDE_TRIP = 4
MOBA_BLOCKS_PER_NARROW_TRIP = 2

V7X_SUBLANES = 8
V7X_LANES = 128
V7X_VMEM_LIMIT_BYTES = 56 * 1024 * 1024

_NT = (((1,), (1,)), ((), ()))
_TN = (((0,), (0,)), ((), ()))


def _params(*semantics):
    return pltpu.CompilerParams(dimension_semantics=semantics,
                                vmem_limit_bytes=V7X_VMEM_LIMIT_BYTES)


def _sigmoid(x):
    return 1.0 / (1.0 + jnp.exp(-x))


def _rmsnorm_kernel(x_ref, g_ref, o_ref):
    x = x_ref[...]
    ms = jnp.mean(x * x, axis=-1, keepdims=True)
    o_ref[...] = (x * lax.rsqrt(ms + RMS_EPS) * g_ref[...]).astype(o_ref.dtype)


def _rmsnorm(x, g, out_dtype, tm=512):
    m, d = x.shape
    return pl.pallas_call(
        _rmsnorm_kernel,
        grid=(m // tm,),
        in_specs=[pl.BlockSpec((tm, d), lambda i: (i, 0)),
                  pl.BlockSpec((1, d), lambda i: (0, 0))],
        out_specs=pl.BlockSpec((tm, d), lambda i: (i, 0)),
        out_shape=jax.ShapeDtypeStruct((m, d), out_dtype),
        compiler_params=_params("arbitrary"),
        name="rmsnorm",
    )(x, g.reshape(1, d).astype(F32))


def _cast_weight_tile(w_ref, wb_ref):
    @pl.when(pl.program_id(1) == 0)
    def _():
        wb_ref[...] = w_ref[...].astype(BF16)


def _matmul_kernel(a_ref, w_ref, o_ref, wb_ref):
    _cast_weight_tile(w_ref, wb_ref)
    o_ref[...] = jnp.dot(a_ref[...], wb_ref[...],
                         preferred_element_type=F32).astype(o_ref.dtype)


def _matmul_residual_kernel(a_ref, w_ref, r_ref, o_ref, wb_ref):
    _cast_weight_tile(w_ref, wb_ref)
    o_ref[...] = (r_ref[...] + jnp.dot(a_ref[...], wb_ref[...],
                                       preferred_element_type=F32)).astype(o_ref.dtype)


def _matmul_silu_kernel(a_ref, w_ref, o_ref, wb_ref):
    _cast_weight_tile(w_ref, wb_ref)
    x = jnp.dot(a_ref[...], wb_ref[...], preferred_element_type=F32)
    o_ref[...] = (x * _sigmoid(x)).astype(o_ref.dtype)


def _matmul_rope_kernel(a_ref, w_ref, cos_ref, sin_ref, o_ref, wb_ref, *, head_dim, key_tile0,
                        key_scale):
    _cast_weight_tile(w_ref, wb_ref)
    a = a_ref[...]
    half = head_dim // 2
    scale = jnp.where(pl.program_id(0) >= key_tile0, key_scale, 1.0)
    for h in range(o_ref.shape[1] // head_dim):
        cols = slice(h * head_dim, (h + 1) * head_dim)
        x = jnp.dot(a, wb_ref[:, cols], preferred_element_type=F32)
        swapped = jnp.concatenate([x[:, half:], x[:, :half]], axis=1)
        rotated = x * cos_ref[...] + swapped * sin_ref[...]
        o_ref[:, cols] = (rotated * scale).astype(o_ref.dtype)


def _matmul(a, w, layer, out_dtype, residual=None, *, tm, tn, name, col_start=0, n_cols=None,
            silu=False, rope=None):
    m, k = a.shape
    n = w.shape[2] - col_start if n_cols is None else n_cols
    tile0 = col_start // tn
    assert col_start % tn == 0 and n % tn == 0 and m % tm == 0
    in_specs = [pl.BlockSpec((tm, k), lambda j, i: (i, 0)),
                pl.BlockSpec((None, k, tn), lambda j, i: (layer, 0, tile0 + j))]
    args = [a, w]
    body = _matmul_kernel
    if residual is not None:
        in_specs.append(pl.BlockSpec((tm, tn), lambda j, i: (i, j)))
        args.append(residual)
        body = _matmul_residual_kernel
    elif silu:
        body = _matmul_silu_kernel
    elif rope is not None:
        head_dim = rope["cos"].shape[1]
        tiles_per_seq = rope["seq"] // tm
        assert rope["seq"] % tm == 0 and tn % head_dim == 0 and rope["key_col"] % tn == 0
        table_spec = pl.BlockSpec((tm, head_dim), lambda j, i: (i % tiles_per_seq, 0))
        in_specs += [table_spec, table_spec]
        args += [rope["cos"], rope["sin_signed"]]
        body = functools.partial(_matmul_rope_kernel, head_dim=head_dim,
                                 key_tile0=(rope["key_col"] - col_start) // tn,
                                 key_scale=rope["key_scale"])
    return pl.pallas_call(
        body,
        grid=(n // tn, m // tm),
        in_specs=in_specs,
        out_specs=pl.BlockSpec((tm, tn), lambda j, i: (i, j)),
        out_shape=jax.ShapeDtypeStruct((m, n), out_dtype),
        scratch_shapes=[pltpu.VMEM((k, tn), BF16)],
        compiler_params=_params("arbitrary", "arbitrary"),
        name=name,
    )(*args)


def _matmul_blocked_transpose_kernel(a_ref, w_ref, o_ref, wt_ref, *, dh, blk):
    @pl.when(pl.program_id(1) == 0)
    def _():
        wt_ref[...] = w_ref[...].T.astype(BF16)

    out_t = lax.dot_general(wt_ref[...], a_ref[...], _NT, preferred_element_type=F32)
    for h in range(o_ref.shape[0]):
        for b in range(o_ref.shape[1]):
            o_ref[h, b] = out_t[h * dh:(h + 1) * dh, b * blk:(b + 1) * blk].astype(o_ref.dtype)


def _matmul_blocked_transpose(a, w, layer, out_dtype, *, batch, seq, dh, blk, col_start, tm, tn,
                              name):
    m, k = a.shape
    n = w.shape[2] - col_start
    heads, n_blocks = n // dh, seq // blk
    tile0 = col_start // tn
    tiles_per_seq = seq // tm
    assert col_start % tn == 0 and n % tn == 0 and seq % tm == 0 and tn % dh == 0 and tm % blk == 0
    return pl.pallas_call(
        functools.partial(_matmul_blocked_transpose_kernel, dh=dh, blk=blk),
        grid=(n // tn, m // tm),
        in_specs=[pl.BlockSpec((tm, k), lambda j, i: (i, 0)),
                  pl.BlockSpec((None, k, tn), lambda j, i: (layer, 0, tile0 + j))],
        out_specs=pl.BlockSpec((None, tn // dh, tm // blk, dh, blk),
                               lambda j, i: (i // tiles_per_seq, j, i % tiles_per_seq, 0, 0)),
        out_shape=jax.ShapeDtypeStruct((batch, heads, n_blocks, dh, blk), out_dtype),
        scratch_shapes=[pltpu.VMEM((tn, k), BF16)],
        compiler_params=_params("arbitrary", "arbitrary"),
        name=name,
    )(a, w)


def _retention_kernel(lg_ref, q_ref, k_ref, v_ref, g_ref, gain_ref, o_ref,
                      state_ref, decay_ref, xi_ref, zeta_ref, *, chunk, n_chunks):
    h = pl.program_id(1)
    dv = v_ref.shape[1]
    log_gamma = lg_ref[h]

    @pl.when(pl.program_id(2) == 0)
    def _start_of_sequence():
        state_ref[...] = jnp.zeros_like(state_ref)
        r = lax.broadcasted_iota(jnp.int32, (chunk, chunk), 0)
        c = lax.broadcasted_iota(jnp.int32, (chunk, chunk), 1)
        diff = (r - c).astype(F32)
        decay_ref[...] = jnp.where(diff >= 0, jnp.exp(log_gamma * jnp.maximum(diff, 0.0)), 0.0)
        idx = lax.broadcasted_iota(jnp.int32, (chunk, dv), 0).astype(F32)
        xi_ref[...] = jnp.exp(log_gamma * (idx + 1.0))
        zeta_ref[...] = jnp.exp(log_gamma * (chunk - 1.0 - idx))

    for ci in range(n_chunks):
        rows = pl.ds(ci * chunk, chunk)
        q = q_ref[rows, :]
        k = k_ref[rows, :]
        v = v_ref[rows, :]
        scores = lax.dot_general(q, k, _NT, preferred_element_type=F32) * decay_ref[...]
        inner = jnp.dot(scores.astype(BF16), v, preferred_element_type=F32)
        state = state_ref[...]
        cross = jnp.dot(q, state.astype(BF16), preferred_element_type=F32) * xi_ref[...]
        v_decayed = (v.astype(F32) * zeta_ref[...]).astype(BF16)
        chunk_decay = xi_ref[chunk - 1:chunk, :]
        state_ref[...] = state * chunk_decay + lax.dot_general(
            k, v_decayed, _TN, preferred_element_type=F32)
        o = inner + cross
        mu = jnp.mean(o, axis=-1, keepdims=True)
        d = o - mu
        var = jnp.mean(d * d, axis=-1, keepdims=True)
        normed = d * lax.rsqrt(var + GN_EPS) * gain_ref[...]
        o_ref[rows, :] = (g_ref[rows, :].astype(F32) * normed).astype(o_ref.dtype)


def _retention_core(qk, v, gate, gn_gain, batch, seq):
    m, vwidth = v.shape
    heads = RET_HEADS
    dk = qk.shape[1] // (2 * heads)
    dv = vwidth // heads
    rows = RET_ROWS_PER_STEP
    chunk = RET_CHUNK
    steps = seq // rows
    assert seq % rows == 0 and rows % chunk == 0 and dk % V7X_LANES == 0 and dv % V7X_LANES == 0
    log_gamma = jnp.log1p(-jnp.power(2.0, -5.0 - jnp.arange(heads, dtype=F32)))

    row_map = lambda b, h, t: b * steps + t
    return pl.pallas_call(
        functools.partial(_retention_kernel, chunk=chunk, n_chunks=rows // chunk),
        grid=(batch, heads, steps),
        in_specs=[
            pl.BlockSpec(memory_space=pltpu.SMEM),
            pl.BlockSpec((rows, dk), lambda b, h, t: (row_map(b, h, t), h)),
            pl.BlockSpec((rows, dk), lambda b, h, t: (row_map(b, h, t), heads + h)),
            pl.BlockSpec((rows, dv), lambda b, h, t: (row_map(b, h, t), h)),
            pl.BlockSpec((rows, dv), lambda b, h, t: (row_map(b, h, t), h)),
            pl.BlockSpec((1, dv), lambda b, h, t: (0, h)),
        ],
        out_specs=pl.BlockSpec((rows, dv), lambda b, h, t: (row_map(b, h, t), h)),
        out_shape=jax.ShapeDtypeStruct((m, vwidth), BF16),
        scratch_shapes=[pltpu.VMEM((dk, dv), F32),
                        pltpu.VMEM((chunk, chunk), F32),
                        pltpu.VMEM((chunk, dv), F32),
                        pltpu.VMEM((chunk, dv), F32)],
        compiler_params=_params("arbitrary", "arbitrary", "arbitrary"),
        name="retention_core",
    )(log_gamma, qk, qk, v, gate, gn_gain.reshape(1, vwidth).astype(F32))


def _rope_tables(seq, head_dim):
    half = head_dim // 2
    inv = ROPE_BASE ** (-jnp.arange(half, dtype=F32) / half)
    ang = jnp.arange(seq).astype(F32)[:, None] * inv[None, :]
    cos = jnp.concatenate([jnp.cos(ang), jnp.cos(ang)], axis=-1)
    sin_signed = jnp.concatenate([-jnp.sin(ang), jnp.sin(ang)], axis=-1)
    return cos, sin_signed


def _t5_bucket_table(n_rel, n_buckets, max_dist):
    n = np.arange(n_rel)
    max_exact = n_buckets // 2
    nf = np.maximum(n, max_exact).astype(np.float64)
    large = max_exact + (np.log(nf / max_exact) / math.log(max_dist / max_exact)
                         * (n_buckets - max_exact)).astype(np.int64)
    large = np.minimum(large, n_buckets - 1)
    return np.where(n < max_exact, n, large).astype(np.int32)


def _moba_kernel(tbl_ref, q_ref, qall_ref, k_ref, vt_ref, bucket_ref, o_ref,
                 kmean_ref, bias_ref, far_bias_ref, pen_ref, s_ref, m_ref, l_ref, acc_ref,
                 *, group, n_blocks, blk, dh, topk, n_buckets, far_bucket, scale, wide, narrow):
    hg = pl.program_id(1)
    qb = pl.program_id(2)
    heads = range(group)
    seq = n_blocks * blk
    slot_prev, slot_own = n_blocks - 2, n_blocks - 1
    to_log2 = math.log2(math.e)

    def cols(g):
        return slice(g * dh, (g + 1) * dh)

    @pl.when(qb == 0)
    def _start_of_heads():
        bucket_of_rel = bucket_ref[...]
        key_pos = lax.broadcasted_iota(jnp.int32, (blk, blk), 0)
        query_pos = lax.broadcasted_iota(jnp.int32, (blk, blk), 1)
        block_id = lax.broadcasted_iota(jnp.int32, (n_blocks, blk), 0)
        for g in heads:
            for j in range(n_blocks):
                kj = k_ref[pl.ds(j * blk, blk), cols(g)].astype(F32)
                kmean_ref[g, pl.ds(j, 1), :] = jnp.mean(kj, axis=0, keepdims=True)
            kmean = kmean_ref[g]
            piece0 = kmean.astype(BF16)
            rest = kmean - piece0.astype(F32)
            piece1 = rest.astype(BF16)
            piece2 = (rest - piece1.astype(F32)).astype(BF16)
            for i in range(n_blocks):
                qi = qall_ref[pl.ds(i * blk, blk), cols(g)]
                gate = (lax.dot_general(piece0, qi, _NT, preferred_element_type=F32)
                        + lax.dot_general(piece1, qi, _NT, preferred_element_type=F32)
                        + lax.dot_general(piece2, qi, _NT, preferred_element_type=F32))
                past = block_id < i
                gate = jnp.where(past, gate, NEG_INF)
                chosen = jnp.zeros((n_blocks, blk), jnp.bool_)
                for _ in range(topk):
                    best = jnp.max(gate, axis=0, keepdims=True)
                    first = jnp.min(jnp.where(gate == best, block_id, n_blocks), axis=0,
                                    keepdims=True)
                    pick = block_id == first
                    chosen = chosen | pick
                    gate = jnp.where(pick, -jnp.inf, gate)
                pen_ref[g, i] = jnp.where(chosen & past, 0.0, NEG_INF)

            bias_of_rel = jnp.zeros(bucket_of_rel.shape, F32)
            for b in range(n_buckets):
                bias_of_rel = jnp.where(bucket_of_rel == b, tbl_ref[hg * group + g, b], bias_of_rel)
            bias_of_rel = bias_of_rel * to_log2
            toeplitz = pltpu.roll(jnp.broadcast_to(bias_of_rel[0:1, :], (blk, 2 * blk)), 0, 1,
                                  stride=1, stride_axis=0)
            bias_ref[g, 0] = toeplitz[:, blk:]
            bias_ref[g, 1] = jnp.where(key_pos <= query_pos, toeplitz[:, :blk], NEG_INF)
            far_bias_ref[g] = jnp.full((1, blk), tbl_ref[hg * group + g, far_bucket], F32) * to_log2

    def qk(g, block):
        kj = k_ref[pl.ds(pl.multiple_of(block * blk, blk), blk), cols(g)]
        return lax.dot_general(kj, q_ref[:, cols(g)], _NT, preferred_element_type=F32)

    def col_max(s):
        return jnp.max(s, axis=0, keepdims=True)

    prev_block = jnp.maximum(qb - 1, 0)
    near_dots = [(qk(g, prev_block), qk(g, qb)) for g in heads]
    for g in heads:
        d_prev, d_own = near_dots[g]
        s_prev = d_prev * (scale * to_log2) + bias_ref[g, 0] + pen_ref[g, qb, pl.ds(prev_block, 1), :]
        s_own = d_own * (scale * to_log2) + bias_ref[g, 1]
        s_ref[g, slot_prev] = s_prev
        s_ref[g, slot_own] = s_own
        m_ref[g] = jnp.maximum(col_max(s_prev), col_max(s_own))

    n_far = jnp.maximum(qb - 1, 0)
    n_wide_trips = n_far // wide
    narrow_start = n_wide_trips * wide
    n_narrow_trips = (n_far - narrow_start + narrow - 1) // narrow

    def far_blocks(start, width):
        return [(start + e, jnp.minimum(start + e, n_far - 1), start + e < n_far)
                for e in range(width)]

    def far_pass_a(blocks):
        dots = [[qk(g, block) for _, block, _ in blocks] for g in heads]
        for g in heads:
            m = m_ref[g]
            for (slot, block, real), d in zip(blocks, dots[g]):
                penalty = jnp.where(real, pen_ref[g, qb, pl.ds(block, 1), :], NEG_INF)
                s = d * (scale * to_log2) + (penalty + far_bias_ref[g])
                s_ref[g, slot] = s
                m = jnp.maximum(m, col_max(s))
            m_ref[g] = m

    def far_loops(body):
        def wide_trip(i, carry):
            body(far_blocks(i * wide, wide))
            return carry

        def narrow_trip(i, carry):
            body(far_blocks(narrow_start + i * narrow, narrow))
            return carry

        lax.fori_loop(0, n_wide_trips, wide_trip, 0)
        lax.fori_loop(0, n_narrow_trips, narrow_trip, 0)

    far_loops(far_pass_a)

    def probabilities(g, slots):
        m = m_ref[g]
        ps = [jnp.exp2(s_ref[g, slot] - m) for slot in slots]
        total = ps[0].sum(axis=0, keepdims=True)
        for p in ps[1:]:
            total = total + p.sum(axis=0, keepdims=True)
        return total, [p.astype(BF16) for p in ps]

    near = [probabilities(g, (slot_prev, slot_own)) for g in heads]
    for g in heads:
        total, (p_prev, p_own) = near[g]
        l_ref[g] = total
        acc_ref[g] = (jnp.dot(vt_ref[g, prev_block], p_prev, preferred_element_type=F32)
                      + jnp.dot(vt_ref[g, qb], p_own, preferred_element_type=F32))

    def far_pass_b(blocks):
        far = [probabilities(g, [slot for slot, _, _ in blocks]) for g in heads]
        for g in heads:
            total, ps = far[g]
            l_ref[g] = l_ref[g] + total
            acc = acc_ref[g]
            for (_, block, _), p in zip(blocks, ps):
                acc = acc + jnp.dot(vt_ref[g, block], p, preferred_element_type=F32)
            acc_ref[g] = acc

    far_loops(far_pass_b)

    for g in heads:
        o_ref[:, cols(g)] = (acc_ref[g] / l_ref[g]).T.astype(o_ref.dtype)


def _moba_core(qk, vt, rel_bias, batch, seq):
    m, width = qk.shape
    d_model = width // 2
    n_buckets, heads = rel_bias.shape
    dh = d_model // heads
    blk = MOBA_BLOCK
    group = MOBA_HEADS_PER_STEP
    n_blocks = seq // blk
    assert seq % blk == 0 and dh % V7X_LANES == 0 and heads % group == 0
    assert blk & (blk - 1) == 0 and n_blocks >= 2

    buckets = _t5_bucket_table(max(seq, 2 * blk), n_buckets, REL_MAX_DIST)
    far_bucket = int(buckets[blk + 1])
    assert np.all(buckets[blk + 1:] == far_bucket), "blocks two or more back must share one bucket"
    bucket_of_rel = np.broadcast_to(buckets[None, :2 * blk], (V7X_SUBLANES, 2 * blk))

    gw = group * dh
    n_groups = heads // group
    return pl.pallas_call(
        functools.partial(_moba_kernel, group=group, n_blocks=n_blocks, blk=blk, dh=dh,
                          wide=MOBA_BLOCKS_PER_WIDE_TRIP, narrow=MOBA_BLOCKS_PER_NARROW_TRIP,
                          topk=min(MOBA_TOPK, n_blocks), n_buckets=n_buckets,
                          far_bucket=far_bucket, scale=dh ** -0.5),
        grid=(batch, n_groups, n_blocks),
        in_specs=[
            pl.BlockSpec(memory_space=pltpu.SMEM),
            pl.BlockSpec((blk, gw), lambda b, h, i: (b * n_blocks + i, h)),
            pl.BlockSpec((seq, gw), lambda b, h, i: (b, h)),
            pl.BlockSpec((seq, gw), lambda b, h, i: (b, n_groups + h)),
            pl.BlockSpec((None, group, n_blocks, dh, blk), lambda b, h, i: (b, h, 0, 0, 0)),
            pl.BlockSpec((V7X_SUBLANES, 2 * blk), lambda b, h, i: (0, 0)),
        ],
        out_specs=pl.BlockSpec((blk, gw), lambda b, h, i: (b * n_blocks + i, h)),
        out_shape=jax.ShapeDtypeStruct((m, d_model), BF16),
        scratch_shapes=[pltpu.VMEM((group, n_blocks, dh), F32),
                        pltpu.VMEM((group, 2, blk, blk), F32),
                        pltpu.VMEM((group, 1, blk), F32),
                        pltpu.VMEM((group, n_blocks, n_blocks, blk), F32),
                        pltpu.VMEM((group, n_blocks, blk, blk), F32),
                        pltpu.VMEM((group, 1, blk), F32),
                        pltpu.VMEM((group, 1, blk), F32),
                        pltpu.VMEM((group, dh, blk), F32)],
        compiler_params=_params("arbitrary", "arbitrary", "arbitrary"),
        name="moba_core",
    )(rel_bias.T.astype(F32), qk, qk, qk, vt, jnp.asarray(bucket_of_rel))


def _ffn_up_kernel(x_ref, wg_ref, wv_ref, cwg_ref, cwv_ref, cbg_ref, cbv_ref, o_ref,
                   wgb_ref, wvb_ref, carry_ref, *, tiles_per_seq):
    _cast_weight_tile(wg_ref, wgb_ref)
    _cast_weight_tile(wv_ref, wvb_ref)

    @pl.when(pl.program_id(1) % tiles_per_seq == 0)
    def _start_of_sequence():
        carry_ref[...] = jnp.zeros_like(carry_ref)

    x = x_ref[...]
    tm = x.shape[0]
    tn = o_ref.shape[1]
    row = lax.broadcasted_iota(jnp.int32, (tm, tn), 0)

    def conv_branch(w_ref, cw_ref, cb_ref, slot):
        u = jnp.dot(x, w_ref[...], preferred_element_type=F32)
        tail = carry_ref[slot]
        prev1 = tail[V7X_SUBLANES - 1:V7X_SUBLANES, :]
        prev2 = tail[V7X_SUBLANES - 2:V7X_SUBLANES - 1, :]
        back1 = jnp.where(row == 0, prev1, pltpu.roll(u, 1, axis=0))
        back2 = jnp.where(row == 0, prev2, jnp.where(row == 1, prev1, pltpu.roll(u, 2, axis=0)))
        carry_ref[slot] = u[tm - V7X_SUBLANES:, :]
        cw = cw_ref[...]
        return cw[2:3, :] * u + cw[1:2, :] * back1 + cw[0:1, :] * back2 + cb_ref[...]

    gate = conv_branch(wgb_ref, cwg_ref, cbg_ref, 0)
    val = conv_branch(wvb_ref, cwv_ref, cbv_ref, 1)
    o_ref[...] = (gate * _sigmoid(gate) * val).astype(o_ref.dtype)


def _ffn_up(x, w_up, conv_w, conv_b, layer, seq, *, tm, tn):
    m, k = x.shape
    d_ff = w_up.shape[2] // 2
    n_col = d_ff // tn
    assert conv_w.shape[1] == CONV_WIDTH == 3 and seq % tm == 0 and d_ff % tn == 0
    return pl.pallas_call(
        functools.partial(_ffn_up_kernel, tiles_per_seq=seq // tm),
        grid=(n_col, m // tm),
        in_specs=[
            pl.BlockSpec((tm, k), lambda j, i: (i, 0)),
            pl.BlockSpec((None, k, tn), lambda j, i: (layer, 0, j)),
            pl.BlockSpec((None, k, tn), lambda j, i: (layer, 0, n_col + j)),
            pl.BlockSpec((None, CONV_WIDTH, tn), lambda j, i: (layer, 0, j)),
            pl.BlockSpec((None, CONV_WIDTH, tn), lambda j, i: (layer, 0, n_col + j)),
            pl.BlockSpec((None, 1, tn), lambda j, i: (layer, 0, j)),
            pl.BlockSpec((None, 1, tn), lambda j, i: (layer, 0, n_col + j)),
        ],
        out_specs=pl.BlockSpec((tm, tn), lambda j, i: (i, j)),
        out_shape=jax.ShapeDtypeStruct((m, d_ff), BF16),
        scratch_shapes=[pltpu.VMEM((k, tn), BF16),
                        pltpu.VMEM((k, tn), BF16),
                        pltpu.VMEM((2, V7X_SUBLANES, tn), F32)],
        compiler_params=_params("arbitrary", "arbitrary"),
        name="ffn_up_conv_gate",
    )(x, w_up, w_up, conv_w, conv_w, conv_b, conv_b)


def kernel(x, mix_norm, ret_w_in, ret_gn, ret_w_out, moba_w_qkv, moba_w_out, rel_bias,
           ffn_norm, ffn_w_up, ffn_conv_w, ffn_conv_b, ffn_w_down, final_norm):
    batch, seq, d_model = x.shape
    depth = mix_norm.shape[0]
    conv_w = ffn_conv_w.astype(F32)
    conv_b = ffn_conv_b.astype(F32)[:, None, :]
    h = x.reshape(batch * seq, d_model)
    for i in range(depth):
        hn = _rmsnorm(h, mix_norm[i], BF16)
        j = i // N_MIXERS
        if i % N_MIXERS == 0:
            vwidth = ret_gn.shape[1]
            dk = d_model // RET_HEADS
            cos, sin_signed = _rope_tables(seq, dk)
            rope = dict(cos=cos, sin_signed=sin_signed, seq=seq, key_col=d_model,
                        key_scale=dk ** -0.5)
            qk = _matmul(hn, ret_w_in, j, BF16, tm=1024, tn=1024, name="ret_qk_proj",
                         n_cols=2 * d_model, rope=rope)
            v = _matmul(hn, ret_w_in, j, BF16, tm=1024, tn=1024, name="ret_v_proj",
                        col_start=2 * d_model, n_cols=vwidth)
            gate = _matmul(hn, ret_w_in, j, BF16, tm=1024, tn=1024, name="ret_gate_proj",
                           col_start=2 * d_model + vwidth, n_cols=vwidth, silu=True)
            y = _retention_core(qk, v, gate, ret_gn[j], batch, seq)
            h = _matmul(y, ret_w_out, j, F32, h, tm=1024, tn=512, name="ret_out_proj")
        else:
            qk = _matmul(hn, moba_w_qkv, j, BF16, tm=1024, tn=1024, name="moba_qk_proj",
                         n_cols=2 * d_model)
            vt = _matmul_blocked_transpose(
                hn, moba_w_qkv, j, BF16, batch=batch, seq=seq, dh=d_model // rel_bias.shape[1],
                blk=MOBA_BLOCK, col_start=2 * d_model, tm=1024, tn=1024, name="moba_vt_proj")
            o = _moba_core(qk, vt, rel_bias, batch, seq)
            h = _matmul(o, moba_w_out, j, F32, h, tm=1024, tn=1024, name="moba_out_proj")
        hn = _rmsnorm(h, ffn_norm[i], BF16)
        a = _ffn_up(hn, ffn_w_up, conv_w, conv_b, i, seq, tm=1024, tn=512)
        h = _matmul(a, ffn_w_down, i, F32, h, tm=512, tn=512, name="ffn_down_proj")
    return _rmsnorm(h, final_norm, F32).reshape(batch, seq, d_model)
```

```python
import functools
import math

import numpy as np
import jax
import jax.numpy as jnp
from jax import lax
from jax.experimental import pallas as pl
from jax.experimental.pallas import tpu as pltpu

F32 = jnp.float32
BF16 = jnp.bfloat16

N_MIXERS = 2
RET_HEADS = 8
ROPE_BASE = 10000.0
MOBA_BLOCK = 256
MOBA_TOPK = 3
REL_MAX_DIST = 128
CONV_WIDTH = 3
RMS_EPS = 1e-6
GN_EPS = 1e-5
NEG_INF = -1e30

RET_CHUNK = 256
RET_ROWS_PER_STEP = 2048
MOBA_HEADS_PER_STEP = 4
MOBA_BLOCKS_PER_WIDE_TRIP = 4
MOBA_BLOCKS_PER_NARROW_TRIP = 2

V7X_SUBLANES = 8
V7X_LANES = 128
V7X_VMEM_LIMIT_BYTES = 56 * 1024 * 1024

_NT = (((1,), (1,)), ((), ()))
_TN = (((0,), (0,)), ((), ()))


def _params(*semantics):
    return pltpu.CompilerParams(dimension_semantics=semantics,
                                vmem_limit_bytes=V7X_VMEM_LIMIT_BYTES)


def _sigmoid(x):
    return 1.0 / (1.0 + jnp.exp(-x))


def _rmsnorm_kernel(x_ref, g_ref, o_ref):
    x = x_ref[...]
    ms = jnp.mean(x * x, axis=-1, keepdims=True)
    o_ref[...] = (x * lax.rsqrt(ms + RMS_EPS) * g_ref[...]).astype(o_ref.dtype)


def _rmsnorm(x, g, out_dtype, tm=512):
    m, d = x.shape
    return pl.pallas_call(
        _rmsnorm_kernel,
        grid=(m // tm,),
        in_specs=[pl.BlockSpec((tm, d), lambda i: (i, 0)),
                  pl.BlockSpec((1, d), lambda i: (0, 0))],
        out_specs=pl.BlockSpec((tm, d), lambda i: (i, 0)),
        out_shape=jax.ShapeDtypeStruct((m, d), out_dtype),
        compiler_params=_params("arbitrary"),
        name="rmsnorm",
    )(x, g.reshape(1, d).astype(F32))


def _cast_weight_tile(w_ref, wb_ref):
    @pl.when(pl.program_id(1) == 0)
    def _():
        wb_ref[...] = w_ref[...].astype(BF16)


def _matmul_kernel(a_ref, w_ref, o_ref, wb_ref):
    _cast_weight_tile(w_ref, wb_ref)
    o_ref[...] = jnp.dot(a_ref[...], wb_ref[...],
                         preferred_element_type=F32).astype(o_ref.dtype)


def _matmul_residual_kernel(a_ref, w_ref, r_ref, o_ref, wb_ref):
    _cast_weight_tile(w_ref, wb_ref)
    o_ref[...] = (r_ref[...] + jnp.dot(a_ref[...], wb_ref[...],
                                       preferred_element_type=F32)).astype(o_ref.dtype)


def _matmul_residual_bf16_weight_kernel(a_ref, w_ref, r_ref, o_ref):
    o_ref[...] = (r_ref[...] + jnp.dot(a_ref[...], w_ref[...],
                                       preferred_element_type=F32)).astype(o_ref.dtype)


def _matmul_silu_kernel(a_ref, w_ref, o_ref, wb_ref):
    _cast_weight_tile(w_ref, wb_ref)
    x = jnp.dot(a_ref[...], wb_ref[...], preferred_element_type=F32)
    o_ref[...] = (x * _sigmoid(x)).astype(o_ref.dtype)


def _matmul_rope_kernel(a_ref, w_ref, cos_ref, sin_ref, o_ref, wb_ref, *, head_dim, key_tile0,
                        key_scale):
    _cast_weight_tile(w_ref, wb_ref)
    a = a_ref[...]
    half = head_dim // 2
    scale = jnp.where(pl.program_id(0) >= key_tile0, key_scale, 1.0)
    for h in range(o_ref.shape[1] // head_dim):
        cols = slice(h * head_dim, (h + 1) * head_dim)
        x = jnp.dot(a, wb_ref[:, cols], preferred_element_type=F32)
        swapped = jnp.concatenate([x[:, half:], x[:, :half]], axis=1)
        rotated = x * cos_ref[...] + swapped * sin_ref[...]
        o_ref[:, cols] = (rotated * scale).astype(o_ref.dtype)


def _matmul(a, w, layer, out_dtype, residual=None, *, tm, tn, name, col_start=0, n_cols=None,
            silu=False, rope=None):
    m, k = a.shape
    n = w.shape[2] - col_start if n_cols is None else n_cols
    tile0 = col_start // tn
    assert col_start % tn == 0 and n % tn == 0 and m % tm == 0
    in_specs = [pl.BlockSpec((tm, k), lambda j, i: (i, 0)),
                pl.BlockSpec((None, k, tn), lambda j, i: (layer, 0, tile0 + j))]
    args = [a, w]
    body = _matmul_kernel
    scratch = [pltpu.VMEM((k, tn), BF16)]
    if residual is not None:
        in_specs.append(pl.BlockSpec((tm, tn), lambda j, i: (i, j)))
        args.append(residual)
        body = _matmul_residual_kernel
        if w.dtype == BF16:
            body, scratch = _matmul_residual_bf16_weight_kernel, []
    elif silu:
        body = _matmul_silu_kernel
    elif rope is not None:
        head_dim = rope["cos"].shape[1]
        tiles_per_seq = rope["seq"] // tm
        assert rope["seq"] % tm == 0 and tn % head_dim == 0 and rope["key_col"] % tn == 0
        table_spec = pl.BlockSpec((tm, head_dim), lambda j, i: (i % tiles_per_seq, 0))
        in_specs += [table_spec, table_spec]
        args += [rope["cos"], rope["sin_signed"]]
        body = functools.partial(_matmul_rope_kernel, head_dim=head_dim,
                                 key_tile0=(rope["key_col"] - col_start) // tn,
                                 key_scale=rope["key_scale"])
    return pl.pallas_call(
        body,
        grid=(n // tn, m // tm),
        in_specs=in_specs,
        out_specs=pl.BlockSpec((tm, tn), lambda j, i: (i, j)),
        out_shape=jax.ShapeDtypeStruct((m, n), out_dtype),
        scratch_shapes=scratch,
        compiler_params=_params("arbitrary", "arbitrary"),
        name=name,
    )(*args)


def _matmul_blocked_transpose_kernel(a_ref, w_ref, o_ref, wt_ref, *, dh, blk):
    @pl.when(pl.program_id(1) == 0)
    def _():
        wt_ref[...] = w_ref[...].T.astype(BF16)

    out_t = lax.dot_general(wt_ref[...], a_ref[...], _NT, preferred_element_type=F32)
    for h in range(o_ref.shape[0]):
        for b in range(o_ref.shape[1]):
            o_ref[h, b] = out_t[h * dh:(h + 1) * dh, b * blk:(b + 1) * blk].astype(o_ref.dtype)


def _matmul_blocked_transpose(a, w, layer, out_dtype, *, batch, seq, dh, blk, col_start, tm, tn,
                              name):
    m, k = a.shape
    n = w.shape[2] - col_start
    heads, n_blocks = n // dh, seq // blk
    tile0 = col_start // tn
    tiles_per_seq = seq // tm
    assert col_start % tn == 0 and n % tn == 0 and seq % tm == 0 and tn % dh == 0 and tm % blk == 0
    return pl.pallas_call(
        functools.partial(_matmul_blocked_transpose_kernel, dh=dh, blk=blk),
        grid=(n // tn, m // tm),
        in_specs=[pl.BlockSpec((tm, k), lambda j, i: (i, 0)),
                  pl.BlockSpec((None, k, tn), lambda j, i: (layer, 0, tile0 + j))],
        out_specs=pl.BlockSpec((None, tn // dh, tm // blk, dh, blk),
                               lambda j, i: (i // tiles_per_seq, j, i % tiles_per_seq, 0, 0)),
        out_shape=jax.ShapeDtypeStruct((batch, heads, n_blocks, dh, blk), out_dtype),
        scratch_shapes=[pltpu.VMEM((tn, k), BF16)],
        compiler_params=_params("arbitrary", "arbitrary"),
        name=name,
    )(a, w)


def _retention_kernel(lg_ref, q_ref, k_ref, v_ref, g_ref, gain_ref, o_ref,
                      state_ref, decay_ref, xi_ref, zeta_ref, *, chunk, n_chunks):
    h = pl.program_id(1)
    dv = v_ref.shape[1]
    log_gamma = lg_ref[h]

    @pl.when(pl.program_id(2) == 0)
    def _start_of_sequence():
        state_ref[...] = jnp.zeros_like(state_ref)
        r = lax.broadcasted_iota(jnp.int32, (chunk, chunk), 0)
        c = lax.broadcasted_iota(jnp.int32, (chunk, chunk), 1)
        diff = (r - c).astype(F32)
        decay_ref[...] = jnp.where(diff >= 0, jnp.exp(log_gamma * jnp.maximum(diff, 0.0)), 0.0)
        idx = lax.broadcasted_iota(jnp.int32, (chunk, dv), 0).astype(F32)
        xi_ref[...] = jnp.exp(log_gamma * (idx + 1.0))
        zeta_ref[...] = jnp.exp(log_gamma * (chunk - 1.0 - idx))

    for ci in range(n_chunks):
        rows = pl.ds(ci * chunk, chunk)
        q = q_ref[rows, :]
        k = k_ref[rows, :]
        v = v_ref[rows, :]
        scores = lax.dot_general(q, k, _NT, preferred_element_type=F32) * decay_ref[...]
        inner = jnp.dot(scores.astype(BF16), v, preferred_element_type=F32)
        state = state_ref[...]
        cross = jnp.dot(q, state.astype(BF16), preferred_element_type=F32) * xi_ref[...]
        v_decayed = (v.astype(F32) * zeta_ref[...]).astype(BF16)
        chunk_decay = xi_ref[chunk - 1:chunk, :]
        state_ref[...] = state * chunk_decay + lax.dot_general(
            k, v_decayed, _TN, preferred_element_type=F32)
        o = inner + cross
        mu = jnp.mean(o, axis=-1, keepdims=True)
        d = o - mu
        var = jnp.mean(d * d, axis=-1, keepdims=True)
        normed = d * lax.rsqrt(var + GN_EPS) * gain_ref[...]
        o_ref[rows, :] = (g_ref[rows, :].astype(F32) * normed).astype(o_ref.dtype)


def _retention_core(qk, v, gate, gn_gain, batch, seq):
    m, vwidth = v.shape
    heads = RET_HEADS
    dk = qk.shape[1] // (2 * heads)
    dv = vwidth // heads
    rows = RET_ROWS_PER_STEP
    chunk = RET_CHUNK
    steps = seq // rows
    assert seq % rows == 0 and rows % chunk == 0 and dk % V7X_LANES == 0 and dv % V7X_LANES == 0
    log_gamma = jnp.log1p(-jnp.power(2.0, -5.0 - jnp.arange(heads, dtype=F32)))

    row_map = lambda b, h, t: b * steps + t
    return pl.pallas_call(
        functools.partial(_retention_kernel, chunk=chunk, n_chunks=rows // chunk),
        grid=(batch, heads, steps),
        in_specs=[
            pl.BlockSpec(memory_space=pltpu.SMEM),
            pl.BlockSpec((rows, dk), lambda b, h, t: (row_map(b, h, t), h)),
            pl.BlockSpec((rows, dk), lambda b, h, t: (row_map(b, h, t), heads + h)),
            pl.BlockSpec((rows, dv), lambda b, h, t: (row_map(b, h, t), h)),
            pl.BlockSpec((rows, dv), lambda b, h, t: (row_map(b, h, t), h)),
            pl.BlockSpec((1, dv), lambda b, h, t: (0, h)),
        ],
        out_specs=pl.BlockSpec((rows, dv), lambda b, h, t: (row_map(b, h, t), h)),
        out_shape=jax.ShapeDtypeStruct((m, vwidth), BF16),
        scratch_shapes=[pltpu.VMEM((dk, dv), F32),
                        pltpu.VMEM((chunk, chunk), F32),
                        pltpu.VMEM((chunk, dv), F32),
                        pltpu.VMEM((chunk, dv), F32)],
        compiler_params=_params("arbitrary", "arbitrary", "arbitrary"),
        name="retention_core",
    )(log_gamma, qk, qk, v, gate, gn_gain.reshape(1, vwidth).astype(F32))


def _rope_tables(seq, head_dim):
    half = head_dim // 2
    inv = ROPE_BASE ** (-jnp.arange(half, dtype=F32) / half)
    ang = jnp.arange(seq).astype(F32)[:, None] * inv[None, :]
    cos = jnp.concatenate([jnp.cos(ang), jnp.cos(ang)], axis=-1)
    sin_signed = jnp.concatenate([-jnp.sin(ang), jnp.sin(ang)], axis=-1)
    return cos, sin_signed


def _t5_bucket_table(n_rel, n_buckets, max_dist):
    n = np.arange(n_rel)
    max_exact = n_buckets // 2
    nf = np.maximum(n, max_exact).astype(np.float64)
    large = max_exact + (np.log(nf / max_exact) / math.log(max_dist / max_exact)
                         * (n_buckets - max_exact)).astype(np.int64)
    large = np.minimum(large, n_buckets - 1)
    return np.where(n < max_exact, n, large).astype(np.int32)


def _moba_kernel(tbl_ref, q_ref, qall_ref, k_ref, vt_ref, bucket_ref, o_ref,
                 kmean_ref, bias_ref, far_bias_ref, pen_ref, s_ref, m_ref, l_ref, acc_ref,
                 *, group, n_blocks, blk, dh, topk, n_buckets, far_bucket, scale, wide, narrow):
    hg = pl.program_id(1)
    qb = pl.program_id(2)
    heads = range(group)
    seq = n_blocks * blk
    slot_prev, slot_own = n_blocks - 2, n_blocks - 1
    to_log2 = math.log2(math.e)

    def cols(g):
        return slice(g * dh, (g + 1) * dh)

    @pl.when(qb == 0)
    def _start_of_heads():
        bucket_of_rel = bucket_ref[...]
        key_pos = lax.broadcasted_iota(jnp.int32, (blk, blk), 0)
        query_pos = lax.broadcasted_iota(jnp.int32, (blk, blk), 1)
        block_id = lax.broadcasted_iota(jnp.int32, (n_blocks, blk), 0)
        for g in heads:
            for j in range(n_blocks):
                kj = k_ref[pl.ds(j * blk, blk), cols(g)].astype(F32)
                kmean_ref[g, pl.ds(j, 1), :] = jnp.mean(kj, axis=0, keepdims=True)
            kmean = kmean_ref[g]
            piece0 = kmean.astype(BF16)
            rest = kmean - piece0.astype(F32)
            piece1 = rest.astype(BF16)
            piece2 = (rest - piece1.astype(F32)).astype(BF16)
            for i in range(n_blocks):
                qi = qall_ref[pl.ds(i * blk, blk), cols(g)]
                gate = (lax.dot_general(piece0, qi, _NT, preferred_element_type=F32)
                        + lax.dot_general(piece1, qi, _NT, preferred_element_type=F32)
                        + lax.dot_general(piece2, qi, _NT, preferred_element_type=F32))
                past = block_id < i
                gate = jnp.where(past, gate, NEG_INF)
                chosen = jnp.zeros((n_blocks, blk), jnp.bool_)
                for _ in range(topk):
                    best = jnp.max(gate, axis=0, keepdims=True)
                    first = jnp.min(jnp.where(gate == best, block_id, n_blocks), axis=0,
                                    keepdims=True)
                    pick = block_id == first
                    chosen = chosen | pick
                    gate = jnp.where(pick, -jnp.inf, gate)
                pen_ref[g, i] = jnp.where(chosen & past, 0.0, NEG_INF)

            bias_of_rel = jnp.zeros(bucket_of_rel.shape, F32)
            for b in range(n_buckets):
                bias_of_rel = jnp.where(bucket_of_rel == b, tbl_ref[hg * group + g, b], bias_of_rel)
            bias_of_rel = bias_of_rel * to_log2
            toeplitz = pltpu.roll(jnp.broadcast_to(bias_of_rel[0:1, :], (blk, 2 * blk)), 0, 1,
                                  stride=1, stride_axis=0)
            bias_ref[g, 0] = toeplitz[:, blk:]
            bias_ref[g, 1] = jnp.where(key_pos <= query_pos, toeplitz[:, :blk], NEG_INF)
            far_bias_ref[g] = jnp.full((1, blk), tbl_ref[hg * group + g, far_bucket], F32) * to_log2

    def qk(g, block):
        kj = k_ref[pl.ds(pl.multiple_of(block * blk, blk), blk), cols(g)]
        return lax.dot_general(kj, q_ref[:, cols(g)], _NT, preferred_element_type=F32)

    def col_max(s):
        return jnp.max(s, axis=0, keepdims=True)

    prev_block = jnp.maximum(qb - 1, 0)
    near_dots = [(qk(g, prev_block), qk(g, qb)) for g in heads]
    for g in heads:
        d_prev, d_own = near_dots[g]
        s_prev = d_prev * (scale * to_log2) + bias_ref[g, 0] + pen_ref[g, qb, pl.ds(prev_block, 1), :]
        s_own = d_own * (scale * to_log2) + bias_ref[g, 1]
        s_ref[g, slot_prev] = s_prev
        s_ref[g, slot_own] = s_own
        m_ref[g] = jnp.maximum(col_max(s_prev), col_max(s_own))

    n_far = jnp.maximum(qb - 1, 0)
    n_wide_trips = n_far // wide
    narrow_start = n_wide_trips * wide
    n_narrow_trips = (n_far - narrow_start + narrow - 1) // narrow

    def far_blocks(start, width):
        return [(start + e, jnp.minimum(start + e, n_far - 1), start + e < n_far)
                for e in range(width)]

    def far_pass_a(blocks):
        dots = [[qk(g, block) for _, block, _ in blocks] for g in heads]
        for g in heads:
            m = m_ref[g]
            for (slot, block, real), d in zip(blocks, dots[g]):
                penalty = jnp.where(real, pen_ref[g, qb, pl.ds(block, 1), :], NEG_INF)
                s = d * (scale * to_log2) + (penalty + far_bias_ref[g])
                s_ref[g, slot] = s
                m = jnp.maximum(m, col_max(s))
            m_ref[g] = m

    def far_loops(body):
        def wide_trip(i, carry):
            body(far_blocks(i * wide, wide))
            return carry

        def narrow_trip(i, carry):
            body(far_blocks(narrow_start + i * narrow, narrow))
            return carry

        lax.fori_loop(0, n_wide_trips, wide_trip, 0)
        lax.fori_loop(0, n_narrow_trips, narrow_trip, 0)

    far_loops(far_pass_a)

    def probabilities(g, slots):
        m = m_ref[g]
        ps = [jnp.exp2(s_ref[g, slot] - m) for slot in slots]
        total = ps[0].sum(axis=0, keepdims=True)
        for p in ps[1:]:
            total = total + p.sum(axis=0, keepdims=True)
        return total, [p.astype(BF16) for p in ps]

    near = [probabilities(g, (slot_prev, slot_own)) for g in heads]
    for g in heads:
        total, (p_prev, p_own) = near[g]
        l_ref[g] = total
        acc_ref[g] = (jnp.dot(vt_ref[g, prev_block], p_prev, preferred_element_type=F32)
                      + jnp.dot(vt_ref[g, qb], p_own, preferred_element_type=F32))

    def far_pass_b(blocks):
        far = [probabilities(g, [slot for slot, _, _ in blocks]) for g in heads]
        for g in heads:
            total, ps = far[g]
            l_ref[g] = l_ref[g] + total
            acc = acc_ref[g]
            for (_, block, _), p in zip(blocks, ps):
                acc = acc + jnp.dot(vt_ref[g, block], p, preferred_element_type=F32)
            acc_ref[g] = acc

    far_loops(far_pass_b)

    for g in heads:
        o_ref[:, cols(g)] = (acc_ref[g] / l_ref[g]).T.astype(o_ref.dtype)


def _moba_core(qk, vt, rel_bias, batch, seq):
    m, width = qk.shape
    d_model = width // 2
    n_buckets, heads = rel_bias.shape
    dh = d_model // heads
    blk = MOBA_BLOCK
    group = MOBA_HEADS_PER_STEP
    n_blocks = seq // blk
    assert seq % blk == 0 and dh % V7X_LANES == 0 and heads % group == 0
    assert blk & (blk - 1) == 0 and n_blocks >= 2

    buckets = _t5_bucket_table(max(seq, 2 * blk), n_buckets, REL_MAX_DIST)
    far_bucket = int(buckets[blk + 1])
    assert np.all(buckets[blk + 1:] == far_bucket), "blocks two or more back must share one bucket"
    bucket_of_rel = np.broadcast_to(buckets[None, :2 * blk], (V7X_SUBLANES, 2 * blk))

    gw = group * dh
    n_groups = heads // group
    return pl.pallas_call(
        functools.partial(_moba_kernel, group=group, n_blocks=n_blocks, blk=blk, dh=dh,
                          wide=MOBA_BLOCKS_PER_WIDE_TRIP, narrow=MOBA_BLOCKS_PER_NARROW_TRIP,
                          topk=min(MOBA_TOPK, n_blocks), n_buckets=n_buckets,
                          far_bucket=far_bucket, scale=dh ** -0.5),
        grid=(batch, n_groups, n_blocks),
        in_specs=[
            pl.BlockSpec(memory_space=pltpu.SMEM),
            pl.BlockSpec((blk, gw), lambda b, h, i: (b * n_blocks + i, h)),
            pl.BlockSpec((seq, gw), lambda b, h, i: (b, h)),
            pl.BlockSpec((seq, gw), lambda b, h, i: (b, n_groups + h)),
            pl.BlockSpec((None, group, n_blocks, dh, blk), lambda b, h, i: (b, h, 0, 0, 0)),
            pl.BlockSpec((V7X_SUBLANES, 2 * blk), lambda b, h, i: (0, 0)),
        ],
        out_specs=pl.BlockSpec((blk, gw), lambda b, h, i: (b * n_blocks + i, h)),
        out_shape=jax.ShapeDtypeStruct((m, d_model), BF16),
        scratch_shapes=[pltpu.VMEM((group, n_blocks, dh), F32),
                        pltpu.VMEM((group, 2, blk, blk), F32),
                        pltpu.VMEM((group, 1, blk), F32),
                        pltpu.VMEM((group, n_blocks, n_blocks, blk), F32),
                        pltpu.VMEM((group, n_blocks, blk, blk), F32),
                        pltpu.VMEM((group, 1, blk), F32),
                        pltpu.VMEM((group, 1, blk), F32),
                        pltpu.VMEM((group, dh, blk), F32)],
        compiler_params=_params("arbitrary", "arbitrary", "arbitrary"),
        name="moba_core",
    )(rel_bias.T.astype(F32), qk, qk, qk, vt, jnp.asarray(bucket_of_rel))


def _ffn_up_kernel(x_ref, wg_ref, wv_ref, cwg_ref, cwv_ref, cbg_ref, cbv_ref, o_ref,
                   wgb_ref, wvb_ref, carry_ref, *, tiles_per_seq):
    _cast_weight_tile(wg_ref, wgb_ref)
    _cast_weight_tile(wv_ref, wvb_ref)

    @pl.when(pl.program_id(1) % tiles_per_seq == 0)
    def _start_of_sequence():
        carry_ref[...] = jnp.zeros_like(carry_ref)

    x = x_ref[...]
    tm = x.shape[0]
    tn = o_ref.shape[1]
    row = lax.broadcasted_iota(jnp.int32, (tm, tn), 0)

    def conv_branch(w_ref, cw_ref, cb_ref, slot):
        u = jnp.dot(x, w_ref[...], preferred_element_type=F32)
        tail = carry_ref[slot]
        prev1 = tail[V7X_SUBLANES - 1:V7X_SUBLANES, :]
        prev2 = tail[V7X_SUBLANES - 2:V7X_SUBLANES - 1, :]
        back1 = jnp.where(row == 0, prev1, pltpu.roll(u, 1, axis=0))
        back2 = jnp.where(row == 0, prev2, jnp.where(row == 1, prev1, pltpu.roll(u, 2, axis=0)))
        carry_ref[slot] = u[tm - V7X_SUBLANES:, :]
        cw = cw_ref[...]
        return cw[2:3, :] * u + cw[1:2, :] * back1 + cw[0:1, :] * back2 + cb_ref[...]

    gate = conv_branch(wgb_ref, cwg_ref, cbg_ref, 0)
    val = conv_branch(wvb_ref, cwv_ref, cbv_ref, 1)
    o_ref[...] = (gate * _sigmoid(gate) * val).astype(o_ref.dtype)


def _ffn_up(x, w_up, conv_w, conv_b, layer, seq, *, tm, tn):
    m, k = x.shape
    d_ff = w_up.shape[2] // 2
    n_col = d_ff // tn
    assert conv_w.shape[1] == CONV_WIDTH == 3 and seq % tm == 0 and d_ff % tn == 0
    return pl.pallas_call(
        functools.partial(_ffn_up_kernel, tiles_per_seq=seq // tm),
        grid=(n_col, m // tm),
        in_specs=[
            pl.BlockSpec((tm, k), lambda j, i: (i, 0)),
            pl.BlockSpec((None, k, tn), lambda j, i: (layer, 0, j)),
            pl.BlockSpec((None, k, tn), lambda j, i: (layer, 0, n_col + j)),
            pl.BlockSpec((None, CONV_WIDTH, tn), lambda j, i: (layer, 0, j)),
            pl.BlockSpec((None, CONV_WIDTH, tn), lambda j, i: (layer, 0, n_col + j)),
            pl.BlockSpec((None, 1, tn), lambda j, i: (layer, 0, j)),
            pl.BlockSpec((None, 1, tn), lambda j, i: (layer, 0, n_col + j)),
        ],
        out_specs=pl.BlockSpec((tm, tn), lambda j, i: (i, j)),
        out_shape=jax.ShapeDtypeStruct((m, d_ff), BF16),
        scratch_shapes=[pltpu.VMEM((k, tn), BF16),
                        pltpu.VMEM((k, tn), BF16),
                        pltpu.VMEM((2, V7X_SUBLANES, tn), F32)],
        compiler_params=_params("arbitrary", "arbitrary"),
        name="ffn_up_conv_gate",
    )(x, w_up, w_up, conv_w, conv_w, conv_b, conv_b)


def kernel(x, mix_norm, ret_w_in, ret_gn, ret_w_out, moba_w_qkv, moba_w_out, rel_bias,
           ffn_norm, ffn_w_up, ffn_conv_w, ffn_conv_b, ffn_w_down, final_norm):
    batch, seq, d_model = x.shape
    depth = mix_norm.shape[0]
    conv_w = ffn_conv_w.astype(F32)
    conv_b = ffn_conv_b.astype(F32)[:, None, :]
    w_down = ffn_w_down.astype(BF16)
    h = x.reshape(batch * seq, d_model)
    for i in range(depth):
        hn = _rmsnorm(h, mix_norm[i], BF16)
        j = i // N_MIXERS
        if i % N_MIXERS == 0:
            vwidth = ret_gn.shape[1]
            dk = d_model // RET_HEADS
            cos, sin_signed = _rope_tables(seq, dk)
            rope = dict(cos=cos, sin_signed=sin_signed, seq=seq, key_col=d_model,
                        key_scale=dk ** -0.5)
            qk = _matmul(hn, ret_w_in, j, BF16, tm=1024, tn=1024, name="ret_qk_proj",
                         n_cols=2 * d_model, rope=rope)
            v = _matmul(hn, ret_w_in, j, BF16, tm=1024, tn=1024, name="ret_v_proj",
                        col_start=2 * d_model, n_cols=vwidth)
            gate = _matmul(hn, ret_w_in, j, BF16, tm=1024, tn=1024, name="ret_gate_proj",
                           col_start=2 * d_model + vwidth, n_cols=vwidth, silu=True)
            y = _retention_core(qk, v, gate, ret_gn[j], batch, seq)
            h = _matmul(y, ret_w_out, j, F32, h, tm=1024, tn=512, name="ret_out_proj")
        else:
            qk = _matmul(hn, moba_w_qkv, j, BF16, tm=1024, tn=1024, name="moba_qk_proj",
                         n_cols=2 * d_model)
            vt = _matmul_blocked_transpose(
                hn, moba_w_qkv, j, BF16, batch=batch, seq=seq, dh=d_model // rel_bias.shape[1],
                blk=MOBA_BLOCK, col_start=2 * d_model, tm=1024, tn=1024, name="moba_vt_proj")
            o = _moba_core(qk, vt, rel_bias, batch, seq)
            h = _matmul(o, moba_w_out, j, F32, h, tm=1024, tn=1024, name="moba_out_proj")
        hn = _rmsnorm(h, ffn_norm[i], BF16)
        a = _ffn_up(hn, ffn_w_up, conv_w, conv_b, i, seq, tm=1024, tn=512)
        h = _matmul(a, w_down, i, F32, h, tm=512, tn=1024, name="ffn_down_proj")
    return _rmsnorm(h, final_norm, F32).reshape(batch, seq, d_model)
```

```python
import functools
import math

import numpy as np
import jax
import jax.numpy as jnp
from jax import lax
from jax.experimental import pallas as pl
from jax.experimental.pallas import tpu as pltpu

F32 = jnp.float32
BF16 = jnp.bfloat16

N_MIXERS = 2
RET_HEADS = 8
ROPE_BASE = 10000.0
MOBA_BLOCK = 256
MOBA_TOPK = 3
REL_MAX_DIST = 128
CONV_WIDTH = 3
RMS_EPS = 1e-6
GN_EPS = 1e-5
NEG_INF = -1e30

RET_CHUNK = 256
RET_ROWS_PER_STEP = 2048
MOBA_HEADS_PER_STEP = 4
MOBA_BLOCKS_PER_WIDE_TRIP = 4
MOBA_BLOCKS_PER_NARROW_TRIP = 2

V7X_SUBLANES = 8
V7X_LANES = 128
V7X_VMEM_LIMIT_BYTES = 56 * 1024 * 1024

_NT = (((1,), (1,)), ((), ()))
_TN = (((0,), (0,)), ((), ()))


def _params(*semantics):
    return pltpu.CompilerParams(dimension_semantics=semantics,
                                vmem_limit_bytes=V7X_VMEM_LIMIT_BYTES)


def _sigmoid(x):
    return 1.0 / (1.0 + jnp.exp(-x))


def _rmsnorm_kernel(x_ref, g_ref, o_ref):
    x = x_ref[...]
    ms = jnp.mean(x * x, axis=-1, keepdims=True)
    o_ref[...] = (x * lax.rsqrt(ms + RMS_EPS) * g_ref[...]).astype(o_ref.dtype)


def _rmsnorm(x, g, out_dtype, tm=1024):
    m, d = x.shape
    return pl.pallas_call(
        _rmsnorm_kernel,
        grid=(m // tm,),
        in_specs=[pl.BlockSpec((tm, d), lambda i: (i, 0)),
                  pl.BlockSpec((1, d), lambda i: (0, 0))],
        out_specs=pl.BlockSpec((tm, d), lambda i: (i, 0)),
        out_shape=jax.ShapeDtypeStruct((m, d), out_dtype),
        compiler_params=_params("arbitrary"),
        name="rmsnorm",
    )(x, g.reshape(1, d).astype(F32))


def _cast_weight_tile(w_ref, wb_ref):
    @pl.when(pl.program_id(1) == 0)
    def _():
        wb_ref[...] = w_ref[...].astype(BF16)


def _matmul_kernel(a_ref, w_ref, o_ref, wb_ref):
    _cast_weight_tile(w_ref, wb_ref)
    o_ref[...] = jnp.dot(a_ref[...], wb_ref[...],
                         preferred_element_type=F32).astype(o_ref.dtype)


def _matmul_residual_kernel(a_ref, w_ref, r_ref, o_ref, wb_ref):
    _cast_weight_tile(w_ref, wb_ref)
    o_ref[...] = (r_ref[...] + jnp.dot(a_ref[...], wb_ref[...],
                                       preferred_element_type=F32)).astype(o_ref.dtype)


def _matmul_residual_bf16_weight_kernel(a_ref, w_ref, r_ref, o_ref):
    o_ref[...] = (r_ref[...] + jnp.dot(a_ref[...], w_ref[...],
                                       preferred_element_type=F32)).astype(o_ref.dtype)


def _matmul_silu_kernel(a_ref, w_ref, o_ref, wb_ref):
    _cast_weight_tile(w_ref, wb_ref)
    x = jnp.dot(a_ref[...], wb_ref[...], preferred_element_type=F32)
    o_ref[...] = (x * _sigmoid(x)).astype(o_ref.dtype)


def _matmul_rope_kernel(a_ref, w_ref, cos_ref, sin_ref, o_ref, wb_ref, *, head_dim, key_tile0,
                        key_scale):
    _cast_weight_tile(w_ref, wb_ref)
    a = a_ref[...]
    half = head_dim // 2
    scale = jnp.where(pl.program_id(0) >= key_tile0, key_scale, 1.0)
    for h in range(o_ref.shape[1] // head_dim):
        cols = slice(h * head_dim, (h + 1) * head_dim)
        x = jnp.dot(a, wb_ref[:, cols], preferred_element_type=F32)
        swapped = jnp.concatenate([x[:, half:], x[:, :half]], axis=1)
        rotated = x * cos_ref[...] + swapped * sin_ref[...]
        o_ref[:, cols] = (rotated * scale).astype(o_ref.dtype)


def _matmul(a, w, layer, out_dtype, residual=None, *, tm, tn, name, col_start=0, n_cols=None,
            silu=False, rope=None):
    m, k = a.shape
    n = w.shape[2] - col_start if n_cols is None else n_cols
    tile0 = col_start // tn
    assert col_start % tn == 0 and n % tn == 0 and m % tm == 0
    in_specs = [pl.BlockSpec((tm, k), lambda j, i: (i, 0)),
                pl.BlockSpec((None, k, tn), lambda j, i: (layer, 0, tile0 + j))]
    args = [a, w]
    body = _matmul_kernel
    scratch = [pltpu.VMEM((k, tn), BF16)]
    if residual is not None:
        in_specs.append(pl.BlockSpec((tm, tn), lambda j, i: (i, j)))
        args.append(residual)
        body = _matmul_residual_kernel
        if w.dtype == BF16:
            body, scratch = _matmul_residual_bf16_weight_kernel, []
    elif silu:
        body = _matmul_silu_kernel
    elif rope is not None:
        head_dim = rope["cos"].shape[1]
        tiles_per_seq = rope["seq"] // tm
        assert rope["seq"] % tm == 0 and tn % head_dim == 0 and rope["key_col"] % tn == 0
        table_spec = pl.BlockSpec((tm, head_dim), lambda j, i: (i % tiles_per_seq, 0))
        in_specs += [table_spec, table_spec]
        args += [rope["cos"], rope["sin_signed"]]
        body = functools.partial(_matmul_rope_kernel, head_dim=head_dim,
                                 key_tile0=(rope["key_col"] - col_start) // tn,
                                 key_scale=rope["key_scale"])
    return pl.pallas_call(
        body,
        grid=(n // tn, m // tm),
        in_specs=in_specs,
        out_specs=pl.BlockSpec((tm, tn), lambda j, i: (i, j)),
        out_shape=jax.ShapeDtypeStruct((m, n), out_dtype),
        scratch_shapes=scratch,
        compiler_params=_params("arbitrary", "arbitrary"),
        name=name,
    )(*args)


def _matmul_blocked_transpose_kernel(a_ref, w_ref, o_ref, wt_ref, *, dh, blk):
    @pl.when(pl.program_id(1) == 0)
    def _():
        wt_ref[...] = w_ref[...].T.astype(BF16)

    out_t = lax.dot_general(wt_ref[...], a_ref[...], _NT, preferred_element_type=F32)
    for h in range(o_ref.shape[0]):
        for b in range(o_ref.shape[1]):
            o_ref[h, b] = out_t[h * dh:(h + 1) * dh, b * blk:(b + 1) * blk].astype(o_ref.dtype)


def _matmul_blocked_transpose(a, w, layer, out_dtype, *, batch, seq, dh, blk, col_start, tm, tn,
                              name):
    m, k = a.shape
    n = w.shape[2] - col_start
    heads, n_blocks = n // dh, seq // blk
    tile0 = col_start // tn
    tiles_per_seq = seq // tm
    assert col_start % tn == 0 and n % tn == 0 and seq % tm == 0 and tn % dh == 0 and tm % blk == 0
    return pl.pallas_call(
        functools.partial(_matmul_blocked_transpose_kernel, dh=dh, blk=blk),
        grid=(n // tn, m // tm),
        in_specs=[pl.BlockSpec((tm, k), lambda j, i: (i, 0)),
                  pl.BlockSpec((None, k, tn), lambda j, i: (layer, 0, tile0 + j))],
        out_specs=pl.BlockSpec((None, tn // dh, tm // blk, dh, blk),
                               lambda j, i: (i // tiles_per_seq, j, i % tiles_per_seq, 0, 0)),
        out_shape=jax.ShapeDtypeStruct((batch, heads, n_blocks, dh, blk), out_dtype),
        scratch_shapes=[pltpu.VMEM((tn, k), BF16)],
        compiler_params=_params("arbitrary", "arbitrary"),
        name=name,
    )(a, w)


def _retention_kernel(lg_ref, q_ref, k_ref, v_ref, g_ref, gain_ref, o_ref,
                      state_ref, decay_ref, xi_ref, zeta_ref, *, chunk, n_chunks):
    h = pl.program_id(1)
    dv = v_ref.shape[1]
    log_gamma = lg_ref[h]

    @pl.when(pl.program_id(2) == 0)
    def _start_of_sequence():
        state_ref[...] = jnp.zeros_like(state_ref)
        r = lax.broadcasted_iota(jnp.int32, (chunk, chunk), 0)
        c = lax.broadcasted_iota(jnp.int32, (chunk, chunk), 1)
        diff = (r - c).astype(F32)
        decay_ref[...] = jnp.where(diff >= 0, jnp.exp(log_gamma * jnp.maximum(diff, 0.0)), 0.0)
        idx = lax.broadcasted_iota(jnp.int32, (chunk, dv), 0).astype(F32)
        xi_ref[...] = jnp.exp(log_gamma * (idx + 1.0))
        zeta_ref[...] = jnp.exp(log_gamma * (chunk - 1.0 - idx))

    for ci in range(n_chunks):
        rows = pl.ds(ci * chunk, chunk)
        q = q_ref[rows, :]
        k = k_ref[rows, :]
        v = v_ref[rows, :]
        scores = lax.dot_general(q, k, _NT, preferred_element_type=F32) * decay_ref[...]
        inner = jnp.dot(scores.astype(BF16), v, preferred_element_type=F32)
        state = state_ref[...]
        cross = jnp.dot(q, state.astype(BF16), preferred_element_type=F32) * xi_ref[...]
        v_decayed = (v.astype(F32) * zeta_ref[...]).astype(BF16)
        chunk_decay = xi_ref[chunk - 1:chunk, :]
        state_ref[...] = state * chunk_decay + lax.dot_general(
            k, v_decayed, _TN, preferred_element_type=F32)
        o = inner + cross
        mu = jnp.mean(o, axis=-1, keepdims=True)
        d = o - mu
        var = jnp.mean(d * d, axis=-1, keepdims=True)
        normed = d * lax.rsqrt(var + GN_EPS) * gain_ref[...]
        o_ref[rows, :] = (g_ref[rows, :].astype(F32) * normed).astype(o_ref.dtype)


def _retention_core(qk, v, gate, gn_gain, batch, seq):
    m, vwidth = v.shape
    heads = RET_HEADS
    dk = qk.shape[1] // (2 * heads)
    dv = vwidth // heads
    rows = RET_ROWS_PER_STEP
    chunk = RET_CHUNK
    steps = seq // rows
    assert seq % rows == 0 and rows % chunk == 0 and dk % V7X_LANES == 0 and dv % V7X_LANES == 0
    log_gamma = jnp.log1p(-jnp.power(2.0, -5.0 - jnp.arange(heads, dtype=F32)))

    row_map = lambda b, h, t: b * steps + t
    return pl.pallas_call(
        functools.partial(_retention_kernel, chunk=chunk, n_chunks=rows // chunk),
        grid=(batch, heads, steps),
        in_specs=[
            pl.BlockSpec(memory_space=pltpu.SMEM),
            pl.BlockSpec((rows, dk), lambda b, h, t: (row_map(b, h, t), h)),
            pl.BlockSpec((rows, dk), lambda b, h, t: (row_map(b, h, t), heads + h)),
            pl.BlockSpec((rows, dv), lambda b, h, t: (row_map(b, h, t), h)),
            pl.BlockSpec((rows, dv), lambda b, h, t: (row_map(b, h, t), h)),
            pl.BlockSpec((1, dv), lambda b, h, t: (0, h)),
        ],
        out_specs=pl.BlockSpec((rows, dv), lambda b, h, t: (row_map(b, h, t), h)),
        out_shape=jax.ShapeDtypeStruct((m, vwidth), BF16),
        scratch_shapes=[pltpu.VMEM((dk, dv), F32),
                        pltpu.VMEM((chunk, chunk), F32),
                        pltpu.VMEM((chunk, dv), F32),
                        pltpu.VMEM((chunk, dv), F32)],
        compiler_params=_params("arbitrary", "arbitrary", "arbitrary"),
        name="retention_core",
    )(log_gamma, qk, qk, v, gate, gn_gain.reshape(1, vwidth).astype(F32))


def _rope_tables(seq, head_dim):
    half = head_dim // 2
    inv = ROPE_BASE ** (-jnp.arange(half, dtype=F32) / half)
    ang = jnp.arange(seq).astype(F32)[:, None] * inv[None, :]
    cos = jnp.concatenate([jnp.cos(ang), jnp.cos(ang)], axis=-1)
    sin_signed = jnp.concatenate([-jnp.sin(ang), jnp.sin(ang)], axis=-1)
    return cos, sin_signed


def _t5_bucket_table(n_rel, n_buckets, max_dist):
    n = np.arange(n_rel)
    max_exact = n_buckets // 2
    nf = np.maximum(n, max_exact).astype(np.float64)
    large = max_exact + (np.log(nf / max_exact) / math.log(max_dist / max_exact)
                         * (n_buckets - max_exact)).astype(np.int64)
    large = np.minimum(large, n_buckets - 1)
    return np.where(n < max_exact, n, large).astype(np.int32)


def _moba_kernel(tbl_ref, q_ref, qall_ref, k_ref, vt_ref, bucket_ref, o_ref,
                 kmean_ref, bias_ref, far_bias_ref, pen_ref, s_ref, m_ref, l_ref, acc_ref,
                 *, group, n_blocks, blk, dh, topk, n_buckets, far_bucket, scale, wide, narrow):
    hg = pl.program_id(1)
    qb = pl.program_id(2)
    heads = range(group)
    seq = n_blocks * blk
    slot_prev, slot_own = n_blocks - 2, n_blocks - 1
    to_log2 = math.log2(math.e)

    def cols(g):
        return slice(g * dh, (g + 1) * dh)

    @pl.when(qb == 0)
    def _start_of_heads():
        bucket_of_rel = bucket_ref[...]
        key_pos = lax.broadcasted_iota(jnp.int32, (blk, blk), 0)
        query_pos = lax.broadcasted_iota(jnp.int32, (blk, blk), 1)
        block_id = lax.broadcasted_iota(jnp.int32, (n_blocks, blk), 0)
        for g in heads:
            for j in range(n_blocks):
                kj = k_ref[pl.ds(j * blk, blk), cols(g)].astype(F32)
                kmean_ref[g, pl.ds(j, 1), :] = jnp.mean(kj, axis=0, keepdims=True)
            kmean = kmean_ref[g]
            piece0 = kmean.astype(BF16)
            rest = kmean - piece0.astype(F32)
            piece1 = rest.astype(BF16)
            piece2 = (rest - piece1.astype(F32)).astype(BF16)
            for i in range(n_blocks):
                qi = qall_ref[pl.ds(i * blk, blk), cols(g)]
                gate = (lax.dot_general(piece0, qi, _NT, preferred_element_type=F32)
                        + lax.dot_general(piece1, qi, _NT, preferred_element_type=F32)
                        + lax.dot_general(piece2, qi, _NT, preferred_element_type=F32))
                past = block_id < i
                gate = jnp.where(past, gate, NEG_INF)
                chosen = jnp.zeros((n_blocks, blk), jnp.bool_)
                for _ in range(topk):
                    best = jnp.max(gate, axis=0, keepdims=True)
                    first = jnp.min(jnp.where(gate == best, block_id, n_blocks), axis=0,
                                    keepdims=True)
                    pick = block_id == first
                    chosen = chosen | pick
                    gate = jnp.where(pick, -jnp.inf, gate)
                pen_ref[g, i] = jnp.where(chosen & past, 0.0, NEG_INF)

            bias_of_rel = jnp.zeros(bucket_of_rel.shape, F32)
            for b in range(n_buckets):
                bias_of_rel = jnp.where(bucket_of_rel == b, tbl_ref[hg * group + g, b], bias_of_rel)
            bias_of_rel = bias_of_rel * to_log2
            toeplitz = pltpu.roll(jnp.broadcast_to(bias_of_rel[0:1, :], (blk, 2 * blk)), 0, 1,
                                  stride=1, stride_axis=0)
            bias_ref[g, 0] = toeplitz[:, blk:]
            bias_ref[g, 1] = jnp.where(key_pos <= query_pos, toeplitz[:, :blk], NEG_INF)
            far_bias_ref[g] = jnp.full((1, blk), tbl_ref[hg * group + g, far_bucket], F32) * to_log2

    def qk(g, block):
        kj = k_ref[pl.ds(pl.multiple_of(block * blk, blk), blk), cols(g)]
        return lax.dot_general(kj, q_ref[:, cols(g)], _NT, preferred_element_type=F32)

    def col_max(s):
        return jnp.max(s, axis=0, keepdims=True)

    prev_block = jnp.maximum(qb - 1, 0)
    near_dots = [(qk(g, prev_block), qk(g, qb)) for g in heads]
    for g in heads:
        d_prev, d_own = near_dots[g]
        s_prev = d_prev * (scale * to_log2) + bias_ref[g, 0] + pen_ref[g, qb, pl.ds(prev_block, 1), :]
        s_own = d_own * (scale * to_log2) + bias_ref[g, 1]
        s_ref[g, slot_prev] = s_prev
        s_ref[g, slot_own] = s_own
        m_ref[g] = jnp.maximum(col_max(s_prev), col_max(s_own))

    n_far = jnp.maximum(qb - 1, 0)
    n_wide_trips = n_far // wide
    narrow_start = n_wide_trips * wide
    n_narrow_trips = (n_far - narrow_start + narrow - 1) // narrow

    def far_blocks(start, width):
        return [(start + e, jnp.minimum(start + e, n_far - 1), start + e < n_far)
                for e in range(width)]

    def far_pass_a(blocks):
        dots = [[qk(g, block) for _, block, _ in blocks] for g in heads]
        for g in heads:
            m = m_ref[g]
            for (slot, block, real), d in zip(blocks, dots[g]):
                penalty = jnp.where(real, pen_ref[g, qb, pl.ds(block, 1), :], NEG_INF)
                s = d * (scale * to_log2) + (penalty + far_bias_ref[g])
                s_ref[g, slot] = s
                m = jnp.maximum(m, col_max(s))
            m_ref[g] = m

    def far_loops(body):
        def wide_trip(i, carry):
            body(far_blocks(i * wide, wide))
            return carry

        def narrow_trip(i, carry):
            body(far_blocks(narrow_start + i * narrow, narrow))
            return carry

        lax.fori_loop(0, n_wide_trips, wide_trip, 0)
        lax.fori_loop(0, n_narrow_trips, narrow_trip, 0)

    far_loops(far_pass_a)

    def probabilities(g, slots):
        m = m_ref[g]
        ps = [jnp.exp2(s_ref[g, slot] - m) for slot in slots]
        total = ps[0].sum(axis=0, keepdims=True)
        for p in ps[1:]:
            total = total + p.sum(axis=0, keepdims=True)
        return total, [p.astype(BF16) for p in ps]

    near = [probabilities(g, (slot_prev, slot_own)) for g in heads]
    for g in heads:
        total, (p_prev, p_own) = near[g]
        l_ref[g] = total
        acc_ref[g] = (jnp.dot(vt_ref[g, prev_block], p_prev, preferred_element_type=F32)
                      + jnp.dot(vt_ref[g, qb], p_own, preferred_element_type=F32))

    def far_pass_b(blocks):
        far = [probabilities(g, [slot for slot, _, _ in blocks]) for g in heads]
        for g in heads:
            total, ps = far[g]
            l_ref[g] = l_ref[g] + total
            acc = acc_ref[g]
            for (_, block, _), p in zip(blocks, ps):
                acc = acc + jnp.dot(vt_ref[g, block], p, preferred_element_type=F32)
            acc_ref[g] = acc

    far_loops(far_pass_b)

    for g in heads:
        o_ref[:, cols(g)] = (acc_ref[g] / l_ref[g]).T.astype(o_ref.dtype)


def _moba_core(qk, vt, rel_bias, batch, seq):
    m, width = qk.shape
    d_model = width // 2
    n_buckets, heads = rel_bias.shape
    dh = d_model // heads
    blk = MOBA_BLOCK
    group = MOBA_HEADS_PER_STEP
    n_blocks = seq // blk
    assert seq % blk == 0 and dh % V7X_LANES == 0 and heads % group == 0
    assert blk & (blk - 1) == 0 and n_blocks >= 2

    buckets = _t5_bucket_table(max(seq, 2 * blk), n_buckets, REL_MAX_DIST)
    far_bucket = int(buckets[blk + 1])
    assert np.all(buckets[blk + 1:] == far_bucket), "blocks two or more back must share one bucket"
    bucket_of_rel = np.broadcast_to(buckets[None, :2 * blk], (V7X_SUBLANES, 2 * blk))

    gw = group * dh
    n_groups = heads // group
    return pl.pallas_call(
        functools.partial(_moba_kernel, group=group, n_blocks=n_blocks, blk=blk, dh=dh,
                          wide=MOBA_BLOCKS_PER_WIDE_TRIP, narrow=MOBA_BLOCKS_PER_NARROW_TRIP,
                          topk=min(MOBA_TOPK, n_blocks), n_buckets=n_buckets,
                          far_bucket=far_bucket, scale=dh ** -0.5),
        grid=(batch, n_groups, n_blocks),
        in_specs=[
            pl.BlockSpec(memory_space=pltpu.SMEM),
            pl.BlockSpec((blk, gw), lambda b, h, i: (b * n_blocks + i, h)),
            pl.BlockSpec((seq, gw), lambda b, h, i: (b, h)),
            pl.BlockSpec((seq, gw), lambda b, h, i: (b, n_groups + h)),
            pl.BlockSpec((None, group, n_blocks, dh, blk), lambda b, h, i: (b, h, 0, 0, 0)),
            pl.BlockSpec((V7X_SUBLANES, 2 * blk), lambda b, h, i: (0, 0)),
        ],
        out_specs=pl.BlockSpec((blk, gw), lambda b, h, i: (b * n_blocks + i, h)),
        out_shape=jax.ShapeDtypeStruct((m, d_model), BF16),
        scratch_shapes=[pltpu.VMEM((group, n_blocks, dh), F32),
                        pltpu.VMEM((group, 2, blk, blk), F32),
                        pltpu.VMEM((group, 1, blk), F32),
                        pltpu.VMEM((group, n_blocks, n_blocks, blk), F32),
                        pltpu.VMEM((group, n_blocks, blk, blk), F32),
                        pltpu.VMEM((group, 1, blk), F32),
                        pltpu.VMEM((group, 1, blk), F32),
                        pltpu.VMEM((group, dh, blk), F32)],
        compiler_params=_params("arbitrary", "arbitrary", "arbitrary"),
        name="moba_core",
    )(rel_bias.T.astype(F32), qk, qk, qk, vt, jnp.asarray(bucket_of_rel))


def _ffn_up_kernel(x_ref, wg_ref, wv_ref, cwg_ref, cwv_ref, cbg_ref, cbv_ref, o_ref,
                   wgb_ref, wvb_ref, carry_ref, *, tiles_per_seq):
    _cast_weight_tile(wg_ref, wgb_ref)
    _cast_weight_tile(wv_ref, wvb_ref)

    @pl.when(pl.program_id(1) % tiles_per_seq == 0)
    def _start_of_sequence():
        carry_ref[...] = jnp.zeros_like(carry_ref)

    x = x_ref[...]
    tm = x.shape[0]
    tn = o_ref.shape[1]
    row = lax.broadcasted_iota(jnp.int32, (tm, tn), 0)

    def conv_branch(w_ref, cw_ref, cb_ref, slot):
        u = jnp.dot(x, w_ref[...], preferred_element_type=F32)
        tail = carry_ref[slot]
        prev1 = tail[V7X_SUBLANES - 1:V7X_SUBLANES, :]
        prev2 = tail[V7X_SUBLANES - 2:V7X_SUBLANES - 1, :]
        back1 = jnp.where(row == 0, prev1, pltpu.roll(u, 1, axis=0))
        back2 = jnp.where(row == 0, prev2, jnp.where(row == 1, prev1, pltpu.roll(u, 2, axis=0)))
        carry_ref[slot] = u[tm - V7X_SUBLANES:, :]
        cw = cw_ref[...]
        return cw[2:3, :] * u + cw[1:2, :] * back1 + cw[0:1, :] * back2 + cb_ref[...]

    gate = conv_branch(wgb_ref, cwg_ref, cbg_ref, 0)
    val = conv_branch(wvb_ref, cwv_ref, cbv_ref, 1)
    o_ref[...] = (gate * _sigmoid(gate) * val).astype(o_ref.dtype)


def _ffn_up(x, w_up, conv_w, conv_b, layer, seq, *, tm, tn):
    m, k = x.shape
    d_ff = w_up.shape[2] // 2
    n_col = d_ff // tn
    assert conv_w.shape[1] == CONV_WIDTH == 3 and seq % tm == 0 and d_ff % tn == 0
    return pl.pallas_call(
        functools.partial(_ffn_up_kernel, tiles_per_seq=seq // tm),
        grid=(n_col, m // tm),
        in_specs=[
            pl.BlockSpec((tm, k), lambda j, i: (i, 0)),
            pl.BlockSpec((None, k, tn), lambda j, i: (layer, 0, j)),
            pl.BlockSpec((None, k, tn), lambda j, i: (layer, 0, n_col + j)),
            pl.BlockSpec((None, CONV_WIDTH, tn), lambda j, i: (layer, 0, j)),
            pl.BlockSpec((None, CONV_WIDTH, tn), lambda j, i: (layer, 0, n_col + j)),
            pl.BlockSpec((None, 1, tn), lambda j, i: (layer, 0, j)),
            pl.BlockSpec((None, 1, tn), lambda j, i: (layer, 0, n_col + j)),
        ],
        out_specs=pl.BlockSpec((tm, tn), lambda j, i: (i, j)),
        out_shape=jax.ShapeDtypeStruct((m, d_ff), BF16),
        scratch_shapes=[pltpu.VMEM((k, tn), BF16),
                        pltpu.VMEM((k, tn), BF16),
                        pltpu.VMEM((2, V7X_SUBLANES, tn), F32)],
        compiler_params=_params("arbitrary", "arbitrary"),
        name="ffn_up_conv_gate",
    )(x, w_up, w_up, conv_w, conv_w, conv_b, conv_b)


def kernel(x, mix_norm, ret_w_in, ret_gn, ret_w_out, moba_w_qkv, moba_w_out, rel_bias,
           ffn_norm, ffn_w_up, ffn_conv_w, ffn_conv_b, ffn_w_down, final_norm):
    batch, seq, d_model = x.shape
    depth = mix_norm.shape[0]
    conv_w = ffn_conv_w.astype(F32)
    conv_b = ffn_conv_b.astype(F32)[:, None, :]
    w_down = ffn_w_down.astype(BF16)
    h = x.reshape(batch * seq, d_model)
    for i in range(depth):
        hn = _rmsnorm(h, mix_norm[i], BF16)
        j = i // N_MIXERS
        if i % N_MIXERS == 0:
            vwidth = ret_gn.shape[1]
            dk = d_model // RET_HEADS
            cos, sin_signed = _rope_tables(seq, dk)
            rope = dict(cos=cos, sin_signed=sin_signed, seq=seq, key_col=d_model,
                        key_scale=dk ** -0.5)
            qk = _matmul(hn, ret_w_in, j, BF16, tm=1024, tn=1024, name="ret_qk_proj",
                         n_cols=2 * d_model, rope=rope)
            v = _matmul(hn, ret_w_in, j, BF16, tm=2048, tn=1024, name="ret_v_proj",
                        col_start=2 * d_model, n_cols=vwidth)
            gate = _matmul(hn, ret_w_in, j, BF16, tm=1024, tn=1024, name="ret_gate_proj",
                           col_start=2 * d_model + vwidth, n_cols=vwidth, silu=True)
            y = _retention_core(qk, v, gate, ret_gn[j], batch, seq)
            h = _matmul(y, ret_w_out, j, F32, h, tm=1024, tn=512, name="ret_out_proj")
        else:
            qk = _matmul(hn, moba_w_qkv, j, BF16, tm=2048, tn=1024, name="moba_qk_proj",
                         n_cols=2 * d_model)
            vt = _matmul_blocked_transpose(
                hn, moba_w_qkv, j, BF16, batch=batch, seq=seq, dh=d_model // rel_bias.shape[1],
                blk=MOBA_BLOCK, col_start=2 * d_model, tm=1024, tn=1024, name="moba_vt_proj")
            o = _moba_core(qk, vt, rel_bias, batch, seq)
            h = _matmul(o, moba_w_out, j, F32, h, tm=1024, tn=1024, name="moba_out_proj")
        hn = _rmsnorm(h, ffn_norm[i], BF16)
        a = _ffn_up(hn, ffn_w_up, conv_w, conv_b, i, seq, tm=1024, tn=512)
        h = _matmul(a, w_down, i, F32, h, tm=512, tn=1024, name="ffn_down_proj")
    return _rmsnorm(h, final_norm, F32).reshape(batch, seq, d_model)
```

```python
import functools
import math

import numpy as np
import jax
import jax.numpy as jnp
from jax import lax
from jax.experimental import pallas as pl
from jax.experimental.pallas import tpu as pltpu

F32 = jnp.float32
BF16 = jnp.bfloat16

N_MIXERS = 2
RET_HEADS = 8
ROPE_BASE = 10000.0
MOBA_BLOCK = 256
MOBA_TOPK = 3
REL_MAX_DIST = 128
CONV_WIDTH = 3
RMS_EPS = 1e-6
GN_EPS = 1e-5
NEG_INF = -1e30

RET_CHUNK = 256
RET_ROWS_PER_STEP = 2048
MOBA_HEADS_PER_STEP = 4
MOBA_BLOCKS_PER_WIDE_TRIP = 4
MOBA_BLOCKS_PER_NARROW_TRIP = 2

V7X_SUBLANES = 8
V7X_LANES = 128
V7X_VMEM_LIMIT_BYTES = 56 * 1024 * 1024

_NT = (((1,), (1,)), ((), ()))
_TN = (((0,), (0,)), ((), ()))


def _params(*semantics):
    return pltpu.CompilerParams(dimension_semantics=semantics,
                                vmem_limit_bytes=V7X_VMEM_LIMIT_BYTES)


def _sigmoid(x):
    return 1.0 / (1.0 + jnp.exp(-x))


def _rmsnorm_kernel(x_ref, g_ref, o_ref):
    x = x_ref[...]
    ms = jnp.mean(x * x, axis=-1, keepdims=True)
    o_ref[...] = (x * lax.rsqrt(ms + RMS_EPS) * g_ref[...]).astype(o_ref.dtype)


def _rmsnorm(x, g, out_dtype, tm=1024):
    m, d = x.shape
    return pl.pallas_call(
        _rmsnorm_kernel,
        grid=(m // tm,),
        in_specs=[pl.BlockSpec((tm, d), lambda i: (i, 0)),
                  pl.BlockSpec((1, d), lambda i: (0, 0))],
        out_specs=pl.BlockSpec((tm, d), lambda i: (i, 0)),
        out_shape=jax.ShapeDtypeStruct((m, d), out_dtype),
        compiler_params=_params("arbitrary"),
        name="rmsnorm",
    )(x, g.reshape(1, d).astype(F32))


def _cast_weight_tile(w_ref, wb_ref):
    @pl.when(pl.program_id(1) == 0)
    def _():
        wb_ref[...] = w_ref[...].astype(BF16)


def _matmul_kernel(a_ref, w_ref, o_ref, wb_ref):
    _cast_weight_tile(w_ref, wb_ref)
    o_ref[...] = jnp.dot(a_ref[...], wb_ref[...],
                         preferred_element_type=F32).astype(o_ref.dtype)


def _matmul_residual_kernel(a_ref, w_ref, r_ref, o_ref, wb_ref):
    _cast_weight_tile(w_ref, wb_ref)
    o_ref[...] = (r_ref[...] + jnp.dot(a_ref[...], wb_ref[...],
                                       preferred_element_type=F32)).astype(o_ref.dtype)


def _matmul_residual_bf16_weight_kernel(a_ref, w_ref, r_ref, o_ref):
    o_ref[...] = (r_ref[...] + jnp.dot(a_ref[...], w_ref[...],
                                       preferred_element_type=F32)).astype(o_ref.dtype)


def _matmul_silu_kernel(a_ref, w_ref, o_ref, wb_ref):
    _cast_weight_tile(w_ref, wb_ref)
    x = jnp.dot(a_ref[...], wb_ref[...], preferred_element_type=F32)
    o_ref[...] = (x * _sigmoid(x)).astype(o_ref.dtype)


def _matmul_rope_kernel(a_ref, w_ref, cos_ref, sin_ref, o_ref, wb_ref, *, head_dim, key_tile0,
                        key_scale):
    _cast_weight_tile(w_ref, wb_ref)
    a = a_ref[...]
    half = head_dim // 2
    scale = jnp.where(pl.program_id(0) >= key_tile0, key_scale, 1.0)
    for h in range(o_ref.shape[1] // head_dim):
        cols = slice(h * head_dim, (h + 1) * head_dim)
        x = jnp.dot(a, wb_ref[:, cols], preferred_element_type=F32)
        swapped = jnp.concatenate([x[:, half:], x[:, :half]], axis=1)
        rotated = x * cos_ref[...] + swapped * sin_ref[...]
        o_ref[:, cols] = (rotated * scale).astype(o_ref.dtype)


def _matmul(a, w, layer, out_dtype, residual=None, *, tm, tn, name, col_start=0, n_cols=None,
            silu=False, rope=None):
    m, k = a.shape
    n = w.shape[2] - col_start if n_cols is None else n_cols
    tile0 = col_start // tn
    assert col_start % tn == 0 and n % tn == 0 and m % tm == 0
    in_specs = [pl.BlockSpec((tm, k), lambda j, i: (i, 0)),
                pl.BlockSpec((None, k, tn), lambda j, i: (layer, 0, tile0 + j))]
    args = [a, w]
    body = _matmul_kernel
    scratch = [pltpu.VMEM((k, tn), BF16)]
    if residual is not None:
        in_specs.append(pl.BlockSpec((tm, tn), lambda j, i: (i, j)))
        args.append(residual)
        body = _matmul_residual_kernel
        if w.dtype == BF16:
            body, scratch = _matmul_residual_bf16_weight_kernel, []
    elif silu:
        body = _matmul_silu_kernel
    elif rope is not None:
        head_dim = rope["cos"].shape[1]
        tiles_per_seq = rope["seq"] // tm
        assert rope["seq"] % tm == 0 and tn % head_dim == 0 and rope["key_col"] % tn == 0
        table_spec = pl.BlockSpec((tm, head_dim), lambda j, i: (i % tiles_per_seq, 0))
        in_specs += [table_spec, table_spec]
        args += [rope["cos"], rope["sin_signed"]]
        body = functools.partial(_matmul_rope_kernel, head_dim=head_dim,
                                 key_tile0=(rope["key_col"] - col_start) // tn,
                                 key_scale=rope["key_scale"])
    return pl.pallas_call(
        body,
        grid=(n // tn, m // tm),
        in_specs=in_specs,
        out_specs=pl.BlockSpec((tm, tn), lambda j, i: (i, j)),
        out_shape=jax.ShapeDtypeStruct((m, n), out_dtype),
        scratch_shapes=scratch,
        compiler_params=_params("arbitrary", "arbitrary"),
        name=name,
    )(*args)


def _matmul_blocked_transpose_kernel(a_ref, w_ref, o_ref, wt_ref, *, dh, blk):
    @pl.when(pl.program_id(1) == 0)
    def _():
        wt_ref[...] = w_ref[...].T.astype(BF16)

    out_t = lax.dot_general(wt_ref[...], a_ref[...], _NT, preferred_element_type=F32)
    for h in range(o_ref.shape[0]):
        for b in range(o_ref.shape[1]):
            o_ref[h, b] = out_t[h * dh:(h + 1) * dh, b * blk:(b + 1) * blk].astype(o_ref.dtype)


def _matmul_blocked_transpose(a, w, layer, out_dtype, *, batch, seq, dh, blk, col_start, n_cols,
                              tm, tn, name):
    m, k = a.shape
    n = n_cols
    heads, n_blocks = n // dh, seq // blk
    tile0 = col_start // tn
    tiles_per_seq = seq // tm
    assert col_start % tn == 0 and n % tn == 0 and seq % tm == 0 and tn % dh == 0 and tm % blk == 0
    return pl.pallas_call(
        functools.partial(_matmul_blocked_transpose_kernel, dh=dh, blk=blk),
        grid=(n // tn, m // tm),
        in_specs=[pl.BlockSpec((tm, k), lambda j, i: (i, 0)),
                  pl.BlockSpec((None, k, tn), lambda j, i: (layer, 0, tile0 + j))],
        out_specs=pl.BlockSpec((None, tn // dh, tm // blk, dh, blk),
                               lambda j, i: (i // tiles_per_seq, j, i % tiles_per_seq, 0, 0)),
        out_shape=jax.ShapeDtypeStruct((batch, heads, n_blocks, dh, blk), out_dtype),
        scratch_shapes=[pltpu.VMEM((tn, k), BF16)],
        compiler_params=_params("arbitrary", "arbitrary"),
        name=name,
    )(a, w)


def _retention_kernel(lg_ref, q_ref, k_ref, v_ref, g_ref, gain_ref, o_ref,
                      state_ref, decay_ref, xi_ref, zeta_ref, *, chunk, n_chunks):
    h = pl.program_id(1)
    dv = v_ref.shape[1]
    log_gamma = lg_ref[h]

    @pl.when(pl.program_id(2) == 0)
    def _start_of_sequence():
        state_ref[...] = jnp.zeros_like(state_ref)
        r = lax.broadcasted_iota(jnp.int32, (chunk, chunk), 0)
        c = lax.broadcasted_iota(jnp.int32, (chunk, chunk), 1)
        diff = (r - c).astype(F32)
        decay_ref[...] = jnp.where(diff >= 0, jnp.exp(log_gamma * jnp.maximum(diff, 0.0)), 0.0)
        idx = lax.broadcasted_iota(jnp.int32, (chunk, dv), 0).astype(F32)
        xi_ref[...] = jnp.exp(log_gamma * (idx + 1.0))
        zeta_ref[...] = jnp.exp(log_gamma * (chunk - 1.0 - idx))

    for ci in range(n_chunks):
        rows = pl.ds(ci * chunk, chunk)
        q = q_ref[rows, :]
        k = k_ref[rows, :]
        v = v_ref[rows, :]
        scores = lax.dot_general(q, k, _NT, preferred_element_type=F32) * decay_ref[...]
        inner = jnp.dot(scores.astype(BF16), v, preferred_element_type=F32)
        state = state_ref[...]
        cross = jnp.dot(q, state.astype(BF16), preferred_element_type=F32) * xi_ref[...]
        v_decayed = (v.astype(F32) * zeta_ref[...]).astype(BF16)
        chunk_decay = xi_ref[chunk - 1:chunk, :]
        state_ref[...] = state * chunk_decay + lax.dot_general(
            k, v_decayed, _TN, preferred_element_type=F32)
        o = inner + cross
        mu = jnp.mean(o, axis=-1, keepdims=True)
        d = o - mu
        var = jnp.mean(d * d, axis=-1, keepdims=True)
        normed = d * lax.rsqrt(var + GN_EPS) * gain_ref[...]
        o_ref[rows, :] = (g_ref[rows, :].astype(F32) * normed).astype(o_ref.dtype)


def _retention_core(qk, v, gate, gn_gain, batch, seq):
    m, vwidth = v.shape
    heads = RET_HEADS
    dk = qk.shape[1] // (2 * heads)
    dv = vwidth // heads
    rows = RET_ROWS_PER_STEP
    chunk = RET_CHUNK
    steps = seq // rows
    assert seq % rows == 0 and rows % chunk == 0 and dk % V7X_LANES == 0 and dv % V7X_LANES == 0
    log_gamma = jnp.log1p(-jnp.power(2.0, -5.0 - jnp.arange(heads, dtype=F32)))

    row_map = lambda b, h, t: b * steps + t
    return pl.pallas_call(
        functools.partial(_retention_kernel, chunk=chunk, n_chunks=rows // chunk),
        grid=(batch, heads, steps),
        in_specs=[
            pl.BlockSpec(memory_space=pltpu.SMEM),
            pl.BlockSpec((rows, dk), lambda b, h, t: (row_map(b, h, t), h)),
            pl.BlockSpec((rows, dk), lambda b, h, t: (row_map(b, h, t), heads + h)),
            pl.BlockSpec((rows, dv), lambda b, h, t: (row_map(b, h, t), h)),
            pl.BlockSpec((rows, dv), lambda b, h, t: (row_map(b, h, t), h)),
            pl.BlockSpec((1, dv), lambda b, h, t: (0, h)),
        ],
        out_specs=pl.BlockSpec((rows, dv), lambda b, h, t: (row_map(b, h, t), h)),
        out_shape=jax.ShapeDtypeStruct((m, vwidth), BF16),
        scratch_shapes=[pltpu.VMEM((dk, dv), F32),
                        pltpu.VMEM((chunk, chunk), F32),
                        pltpu.VMEM((chunk, dv), F32),
                        pltpu.VMEM((chunk, dv), F32)],
        compiler_params=_params("arbitrary", "arbitrary", "arbitrary"),
        name="retention_core",
    )(log_gamma, qk, qk, v, gate, gn_gain.reshape(1, vwidth).astype(F32))


def _rope_tables(seq, head_dim):
    half = head_dim // 2
    inv = ROPE_BASE ** (-jnp.arange(half, dtype=F32) / half)
    ang = jnp.arange(seq).astype(F32)[:, None] * inv[None, :]
    cos = jnp.concatenate([jnp.cos(ang), jnp.cos(ang)], axis=-1)
    sin_signed = jnp.concatenate([-jnp.sin(ang), jnp.sin(ang)], axis=-1)
    return cos, sin_signed


def _t5_bucket_table(n_rel, n_buckets, max_dist):
    n = np.arange(n_rel)
    max_exact = n_buckets // 2
    nf = np.maximum(n, max_exact).astype(np.float64)
    large = max_exact + (np.log(nf / max_exact) / math.log(max_dist / max_exact)
                         * (n_buckets - max_exact)).astype(np.int64)
    large = np.minimum(large, n_buckets - 1)
    return np.where(n < max_exact, n, large).astype(np.int32)


def _moba_kernel(tbl_ref, qt_ref, qtall_ref, k_ref, vt_ref, bucket_ref, o_ref,
                 kmean_ref, bias_ref, far_bias_ref, pen_ref, s_ref, m_ref, l_ref, acc_ref,
                 *, group, n_blocks, blk, dh, topk, n_buckets, far_bucket, scale, wide, narrow):
    hg = pl.program_id(1)
    qb = pl.program_id(2)
    heads = range(group)
    seq = n_blocks * blk
    slot_prev, slot_own = n_blocks - 2, n_blocks - 1
    to_log2 = math.log2(math.e)

    def cols(g):
        return slice(g * dh, (g + 1) * dh)

    @pl.when(qb == 0)
    def _start_of_heads():
        bucket_of_rel = bucket_ref[...]
        key_pos = lax.broadcasted_iota(jnp.int32, (blk, blk), 0)
        query_pos = lax.broadcasted_iota(jnp.int32, (blk, blk), 1)
        block_id = lax.broadcasted_iota(jnp.int32, (n_blocks, blk), 0)
        for g in heads:
            for j in range(n_blocks):
                kj = k_ref[pl.ds(j * blk, blk), cols(g)].astype(F32)
                kmean_ref[g, pl.ds(j, 1), :] = jnp.mean(kj, axis=0, keepdims=True)
            kmean = kmean_ref[g]
            piece0 = kmean.astype(BF16)
            rest = kmean - piece0.astype(F32)
            piece1 = rest.astype(BF16)
            piece2 = (rest - piece1.astype(F32)).astype(BF16)
            for i in range(n_blocks):
                qt_i = qtall_ref[g, i]
                gate = (jnp.dot(piece0, qt_i, preferred_element_type=F32)
                        + jnp.dot(piece1, qt_i, preferred_element_type=F32)
                        + jnp.dot(piece2, qt_i, preferred_element_type=F32))
                past = block_id < i
                gate = jnp.where(past, gate, NEG_INF)
                chosen = jnp.zeros((n_blocks, blk), jnp.bool_)
                for _ in range(topk):
                    best = jnp.max(gate, axis=0, keepdims=True)
                    first = jnp.min(jnp.where(gate == best, block_id, n_blocks), axis=0,
                                    keepdims=True)
                    pick = block_id == first
                    chosen = chosen | pick
                    gate = jnp.where(pick, -jnp.inf, gate)
                pen_ref[g, i] = jnp.where(chosen & past, 0.0, NEG_INF)

            bias_of_rel = jnp.zeros(bucket_of_rel.shape, F32)
            for b in range(n_buckets):
                bias_of_rel = jnp.where(bucket_of_rel == b, tbl_ref[hg * group + g, b], bias_of_rel)
            bias_of_rel = bias_of_rel * to_log2
            toeplitz = pltpu.roll(jnp.broadcast_to(bias_of_rel[0:1, :], (blk, 2 * blk)), 0, 1,
                                  stride=1, stride_axis=0)
            bias_ref[g, 0] = toeplitz[:, blk:]
            bias_ref[g, 1] = jnp.where(key_pos <= query_pos, toeplitz[:, :blk], NEG_INF)
            far_bias_ref[g] = jnp.full((1, blk), tbl_ref[hg * group + g, far_bucket], F32) * to_log2

    def qk(g, block):
        kj = k_ref[pl.ds(pl.multiple_of(block * blk, blk), blk), cols(g)]
        return jnp.dot(kj, qt_ref[g], preferred_element_type=F32)

    def col_max(s):
        return jnp.max(s, axis=0, keepdims=True)

    prev_block = jnp.maximum(qb - 1, 0)
    near_dots = [(qk(g, prev_block), qk(g, qb)) for g in heads]
    for g in heads:
        d_prev, d_own = near_dots[g]
        s_prev = d_prev * (scale * to_log2) + bias_ref[g, 0] + pen_ref[g, qb, pl.ds(prev_block, 1), :]
        s_own = d_own * (scale * to_log2) + bias_ref[g, 1]
        s_ref[g, slot_prev] = s_prev
        s_ref[g, slot_own] = s_own
        m_ref[g] = jnp.maximum(col_max(s_prev), col_max(s_own))

    n_far = jnp.maximum(qb - 1, 0)
    n_wide_trips = n_far // wide
    narrow_start = n_wide_trips * wide
    n_narrow_trips = (n_far - narrow_start + narrow - 1) // narrow

    def far_blocks(start, width):
        return [(start + e, jnp.minimum(start + e, n_far - 1), start + e < n_far)
                for e in range(width)]

    def far_pass_a(blocks):
        dots = [[qk(g, block) for _, block, _ in blocks] for g in heads]
        for g in heads:
            m = m_ref[g]
            for (slot, block, real), d in zip(blocks, dots[g]):
                penalty = jnp.where(real, pen_ref[g, qb, pl.ds(block, 1), :], NEG_INF)
                s = d * (scale * to_log2) + (penalty + far_bias_ref[g])
                s_ref[g, slot] = s
                m = jnp.maximum(m, col_max(s))
            m_ref[g] = m

    def far_loops(body):
        def wide_trip(i, carry):
            body(far_blocks(i * wide, wide))
            return carry

        def narrow_trip(i, carry):
            body(far_blocks(narrow_start + i * narrow, narrow))
            return carry

        lax.fori_loop(0, n_wide_trips, wide_trip, 0)
        lax.fori_loop(0, n_narrow_trips, narrow_trip, 0)

    far_loops(far_pass_a)

    def probabilities(g, slots):
        m = m_ref[g]
        ps = [jnp.exp2(s_ref[g, slot] - m) for slot in slots]
        total = ps[0].sum(axis=0, keepdims=True)
        for p in ps[1:]:
            total = total + p.sum(axis=0, keepdims=True)
        return total, [p.astype(BF16) for p in ps]

    near = [probabilities(g, (slot_prev, slot_own)) for g in heads]
    for g in heads:
        total, (p_prev, p_own) = near[g]
        l_ref[g] = total
        acc_ref[g] = (jnp.dot(vt_ref[g, prev_block], p_prev, preferred_element_type=F32)
                      + jnp.dot(vt_ref[g, qb], p_own, preferred_element_type=F32))

    def far_pass_b(blocks):
        far = [probabilities(g, [slot for slot, _, _ in blocks]) for g in heads]
        for g in heads:
            total, ps = far[g]
            l_ref[g] = l_ref[g] + total
            acc = acc_ref[g]
            for (_, block, _), p in zip(blocks, ps):
                acc = acc + jnp.dot(vt_ref[g, block], p, preferred_element_type=F32)
            acc_ref[g] = acc

    far_loops(far_pass_b)

    for g in heads:
        o_ref[:, cols(g)] = (acc_ref[g] / l_ref[g]).T.astype(o_ref.dtype)


def _moba_core(qt, k, vt, rel_bias, batch, seq):
    m, d_model = k.shape
    n_buckets, heads = rel_bias.shape
    dh = d_model // heads
    blk = MOBA_BLOCK
    group = MOBA_HEADS_PER_STEP
    n_blocks = seq // blk
    assert seq % blk == 0 and dh % V7X_LANES == 0 and heads % group == 0
    assert blk & (blk - 1) == 0 and n_blocks >= 2

    buckets = _t5_bucket_table(max(seq, 2 * blk), n_buckets, REL_MAX_DIST)
    far_bucket = int(buckets[blk + 1])
    assert np.all(buckets[blk + 1:] == far_bucket), "blocks two or more back must share one bucket"
    bucket_of_rel = np.broadcast_to(buckets[None, :2 * blk], (V7X_SUBLANES, 2 * blk))

    gw = group * dh
    n_groups = heads // group
    return pl.pallas_call(
        functools.partial(_moba_kernel, group=group, n_blocks=n_blocks, blk=blk, dh=dh,
                          wide=MOBA_BLOCKS_PER_WIDE_TRIP, narrow=MOBA_BLOCKS_PER_NARROW_TRIP,
                          topk=min(MOBA_TOPK, n_blocks), n_buckets=n_buckets,
                          far_bucket=far_bucket, scale=dh ** -0.5),
        grid=(batch, n_groups, n_blocks),
        in_specs=[
            pl.BlockSpec(memory_space=pltpu.SMEM),
            pl.BlockSpec((None, group, None, dh, blk), lambda b, h, i: (b, h, i, 0, 0)),
            pl.BlockSpec((None, group, n_blocks, dh, blk), lambda b, h, i: (b, h, 0, 0, 0)),
            pl.BlockSpec((seq, gw), lambda b, h, i: (b, h)),
            pl.BlockSpec((None, group, n_blocks, dh, blk), lambda b, h, i: (b, h, 0, 0, 0)),
            pl.BlockSpec((V7X_SUBLANES, 2 * blk), lambda b, h, i: (0, 0)),
        ],
        out_specs=pl.BlockSpec((blk, gw), lambda b, h, i: (b * n_blocks + i, h)),
        out_shape=jax.ShapeDtypeStruct((m, d_model), BF16),
        scratch_shapes=[pltpu.VMEM((group, n_blocks, dh), F32),
                        pltpu.VMEM((group, 2, blk, blk), F32),
                        pltpu.VMEM((group, 1, blk), F32),
                        pltpu.VMEM((group, n_blocks, n_blocks, blk), F32),
                        pltpu.VMEM((group, n_blocks, blk, blk), F32),
                        pltpu.VMEM((group, 1, blk), F32),
                        pltpu.VMEM((group, 1, blk), F32),
                        pltpu.VMEM((group, dh, blk), F32)],
        compiler_params=_params("arbitrary", "arbitrary", "arbitrary"),
        name="moba_core",
    )(rel_bias.T.astype(F32), qt, qt, k, vt, jnp.asarray(bucket_of_rel))


def _ffn_up_kernel(x_ref, wg_ref, wv_ref, cwg_ref, cwv_ref, cbg_ref, cbv_ref, o_ref,
                   wgb_ref, wvb_ref, carry_ref, *, tiles_per_seq):
    _cast_weight_tile(wg_ref, wgb_ref)
    _cast_weight_tile(wv_ref, wvb_ref)

    @pl.when(pl.program_id(1) % tiles_per_seq == 0)
    def _start_of_sequence():
        carry_ref[...] = jnp.zeros_like(carry_ref)

    x = x_ref[...]
    tm = x.shape[0]
    tn = o_ref.shape[1]
    row = lax.broadcasted_iota(jnp.int32, (tm, tn), 0)

    def conv_branch(w_ref, cw_ref, cb_ref, slot):
        u = jnp.dot(x, w_ref[...], preferred_element_type=F32)
        tail = carry_ref[slot]
        prev1 = tail[V7X_SUBLANES - 1:V7X_SUBLANES, :]
        prev2 = tail[V7X_SUBLANES - 2:V7X_SUBLANES - 1, :]
        back1 = jnp.where(row == 0, prev1, pltpu.roll(u, 1, axis=0))
        back2 = jnp.where(row == 0, prev2, jnp.where(row == 1, prev1, pltpu.roll(u, 2, axis=0)))
        carry_ref[slot] = u[tm - V7X_SUBLANES:, :]
        cw = cw_ref[...]
        return cw[2:3, :] * u + cw[1:2, :] * back1 + cw[0:1, :] * back2 + cb_ref[...]

    gate = conv_branch(wgb_ref, cwg_ref, cbg_ref, 0)
    val = conv_branch(wvb_ref, cwv_ref, cbv_ref, 1)
    o_ref[...] = (gate * _sigmoid(gate) * val).astype(o_ref.dtype)


def _ffn_up(x, w_up, conv_w, conv_b, layer, seq, *, tm, tn):
    m, k = x.shape
    d_ff = w_up.shape[2] // 2
    n_col = d_ff // tn
    assert conv_w.shape[1] == CONV_WIDTH == 3 and seq % tm == 0 and d_ff % tn == 0
    return pl.pallas_call(
        functools.partial(_ffn_up_kernel, tiles_per_seq=seq // tm),
        grid=(n_col, m // tm),
        in_specs=[
            pl.BlockSpec((tm, k), lambda j, i: (i, 0)),
            pl.BlockSpec((None, k, tn), lambda j, i: (layer, 0, j)),
            pl.BlockSpec((None, k, tn), lambda j, i: (layer, 0, n_col + j)),
            pl.BlockSpec((None, CONV_WIDTH, tn), lambda j, i: (layer, 0, j)),
            pl.BlockSpec((None, CONV_WIDTH, tn), lambda j, i: (layer, 0, n_col + j)),
            pl.BlockSpec((None, 1, tn), lambda j, i: (layer, 0, j)),
            pl.BlockSpec((None, 1, tn), lambda j, i: (layer, 0, n_col + j)),
        ],
        out_specs=pl.BlockSpec((tm, tn), lambda j, i: (i, j)),
        out_shape=jax.ShapeDtypeStruct((m, d_ff), BF16),
        scratch_shapes=[pltpu.VMEM((k, tn), BF16),
                        pltpu.VMEM((k, tn), BF16),
                        pltpu.VMEM((2, V7X_SUBLANES, tn), F32)],
        compiler_params=_params("arbitrary", "arbitrary"),
        name="ffn_up_conv_gate",
    )(x, w_up, w_up, conv_w, conv_w, conv_b, conv_b)


def kernel(x, mix_norm, ret_w_in, ret_gn, ret_w_out, moba_w_qkv, moba_w_out, rel_bias,
           ffn_norm, ffn_w_up, ffn_conv_w, ffn_conv_b, ffn_w_down, final_norm):
    batch, seq, d_model = x.shape
    depth = mix_norm.shape[0]
    conv_w = ffn_conv_w.astype(F32)
    conv_b = ffn_conv_b.astype(F32)[:, None, :]
    w_down = ffn_w_down.astype(BF16)
    h = x.reshape(batch * seq, d_model)
    for i in range(depth):
        hn = _rmsnorm(h, mix_norm[i], BF16)
        j = i // N_MIXERS
        if i % N_MIXERS == 0:
            vwidth = ret_gn.shape[1]
            dk = d_model // RET_HEADS
            cos, sin_signed = _rope_tables(seq, dk)
            rope = dict(cos=cos, sin_signed=sin_signed, seq=seq, key_col=d_model,
                        key_scale=dk ** -0.5)
            qk = _matmul(hn, ret_w_in, j, BF16, tm=1024, tn=1024, name="ret_qk_proj",
                         n_cols=2 * d_model, rope=rope)
            v = _matmul(hn, ret_w_in, j, BF16, tm=1024, tn=1024, name="ret_v_proj",
                        col_start=2 * d_model, n_cols=vwidth)
            gate = _matmul(hn, ret_w_in, j, BF16, tm=1024, tn=1024, name="ret_gate_proj",
                           col_start=2 * d_model + vwidth, n_cols=vwidth, silu=True)
            y = _retention_core(qk, v, gate, ret_gn[j], batch, seq)
            h = _matmul(y, ret_w_out, j, F32, h, tm=1024, tn=512, name="ret_out_proj")
        else:
            blocked = dict(batch=batch, seq=seq, dh=d_model // rel_bias.shape[1], blk=MOBA_BLOCK,
                           n_cols=d_model, tm=1024, tn=1024)
            qt = _matmul_blocked_transpose(hn, moba_w_qkv, j, BF16, col_start=0,
                                           name="moba_qt_proj", **blocked)
            k = _matmul(hn, moba_w_qkv, j, BF16, tm=1024, tn=1024, name="moba_k_proj",
                        col_start=d_model, n_cols=d_model)
            vt = _matmul_blocked_transpose(hn, moba_w_qkv, j, BF16, col_start=2 * d_model,
                                           name="moba_vt_proj", **blocked)
            o = _moba_core(qt, k, vt, rel_bias, batch, seq)
            h = _matmul(o, moba_w_out, j, F32, h, tm=1024, tn=1024, name="moba_out_proj")
        hn = _rmsnorm(h, ffn_norm[i], BF16)
        a = _ffn_up(hn, ffn_w_up, conv_w, conv_b, i, seq, tm=1024, tn=512)
        h = _matmul(a, w_down, i, F32, h, tm=512, tn=1024, name="ffn_down_proj")
    return _rmsnorm(h, final_norm, F32).reshape(batch, seq, d_model)
```

```python
import functools
import math

import numpy as np
import jax
import jax.numpy as jnp
from jax import lax
from jax.experimental import pallas as pl
from jax.experimental.pallas import tpu as pltpu

F32 = jnp.float32
BF16 = jnp.bfloat16

N_MIXERS = 2
RET_HEADS = 8
ROPE_BASE = 10000.0
MOBA_BLOCK = 256
MOBA_TOPK = 3
REL_MAX_DIST = 128
CONV_WIDTH = 3
RMS_EPS = 1e-6
GN_EPS = 1e-5
NEG_INF = -1e30

RET_CHUNK = 256
RET_ROWS_PER_STEP = 4096
MOBA_HEADS_PER_STEP = 4
MOBA_BLOCKS_PER_WIDE_TRIP = 4
MOBA_BLOCKS_PER_NARROW_TRIP = 2

V7X_SUBLANES = 8
V7X_LANES = 128
V7X_VMEM_LIMIT_BYTES = 56 * 1024 * 1024

_TILES = {
    "projection": dict(tm=1024, tn=1024),
    "ret_out_proj": dict(tm=512, tn=1024),
    "moba_out_proj": dict(tm=512, tn=2048),
    "ffn_up": dict(tm=1024, tn=512),
    "ffn_down_proj": dict(tm=512, tn=1024),
}

_NT = (((1,), (1,)), ((), ()))
_TN = (((0,), (0,)), ((), ()))


def _params(*semantics):
    return pltpu.CompilerParams(dimension_semantics=semantics,
                                vmem_limit_bytes=V7X_VMEM_LIMIT_BYTES)


def _sigmoid(x):
    return 1.0 / (1.0 + jnp.exp(-x))


def _rmsnorm_kernel(x_ref, g_ref, o_ref):
    x = x_ref[...]
    ms = jnp.mean(x * x, axis=-1, keepdims=True)
    o_ref[...] = (x * lax.rsqrt(ms + RMS_EPS) * g_ref[...]).astype(o_ref.dtype)


def _rmsnorm(x, g, out_dtype, tm=1024):
    m, d = x.shape
    return pl.pallas_call(
        _rmsnorm_kernel,
        grid=(m // tm,),
        in_specs=[pl.BlockSpec((tm, d), lambda i: (i, 0)),
                  pl.BlockSpec((1, d), lambda i: (0, 0))],
        out_specs=pl.BlockSpec((tm, d), lambda i: (i, 0)),
        out_shape=jax.ShapeDtypeStruct((m, d), out_dtype),
        compiler_params=_params("arbitrary"),
        name="rmsnorm",
    )(x, g.reshape(1, d).astype(F32))


def _cast_weight_tile(w_ref, wb_ref):
    @pl.when(pl.program_id(1) == 0)
    def _():
        wb_ref[...] = w_ref[...].astype(BF16)


def _matmul_kernel(a_ref, w_ref, o_ref, wb_ref):
    _cast_weight_tile(w_ref, wb_ref)
    o_ref[...] = jnp.dot(a_ref[...], wb_ref[...],
                         preferred_element_type=F32).astype(o_ref.dtype)


def _matmul_residual_kernel(a_ref, w_ref, r_ref, o_ref, wb_ref):
    _cast_weight_tile(w_ref, wb_ref)
    o_ref[...] = (r_ref[...] + jnp.dot(a_ref[...], wb_ref[...],
                                       preferred_element_type=F32)).astype(o_ref.dtype)


def _matmul_residual_bf16_weight_kernel(a_ref, w_ref, r_ref, o_ref):
    o_ref[...] = (r_ref[...] + jnp.dot(a_ref[...], w_ref[...],
                                       preferred_element_type=F32)).astype(o_ref.dtype)


def _matmul_silu_kernel(a_ref, w_ref, o_ref, wb_ref):
    _cast_weight_tile(w_ref, wb_ref)
    x = jnp.dot(a_ref[...], wb_ref[...], preferred_element_type=F32)
    o_ref[...] = (x * _sigmoid(x)).astype(o_ref.dtype)


def _matmul_rope_kernel(a_ref, w_ref, cos_ref, sin_ref, o_ref, wb_ref, *, head_dim, key_tile0,
                        key_scale):
    _cast_weight_tile(w_ref, wb_ref)
    a = a_ref[...]
    half = head_dim // 2
    scale = jnp.where(pl.program_id(0) >= key_tile0, key_scale, 1.0)
    for h in range(o_ref.shape[1] // head_dim):
        cols = slice(h * head_dim, (h + 1) * head_dim)
        x = jnp.dot(a, wb_ref[:, cols], preferred_element_type=F32)
        swapped = jnp.concatenate([x[:, half:], x[:, :half]], axis=1)
        rotated = x * cos_ref[...] + swapped * sin_ref[...]
        o_ref[:, cols] = (rotated * scale).astype(o_ref.dtype)


def _matmul(a, w, layer, out_dtype, residual=None, *, tm, tn, name, col_start=0, n_cols=None,
            silu=False, rope=None):
    m, k = a.shape
    n = w.shape[2] - col_start if n_cols is None else n_cols
    tile0 = col_start // tn
    assert col_start % tn == 0 and n % tn == 0 and m % tm == 0
    in_specs = [pl.BlockSpec((tm, k), lambda j, i: (i, 0)),
                pl.BlockSpec((None, k, tn), lambda j, i: (layer, 0, tile0 + j))]
    args = [a, w]
    body = _matmul_kernel
    scratch = [pltpu.VMEM((k, tn), BF16)]
    if residual is not None:
        in_specs.append(pl.BlockSpec((tm, tn), lambda j, i: (i, j)))
        args.append(residual)
        body = _matmul_residual_kernel
        if w.dtype == BF16:
            body, scratch = _matmul_residual_bf16_weight_kernel, []
    elif silu:
        body = _matmul_silu_kernel
    elif rope is not None:
        head_dim = rope["cos"].shape[1]
        tiles_per_seq = rope["seq"] // tm
        assert rope["seq"] % tm == 0 and tn % head_dim == 0 and rope["key_col"] % tn == 0
        table_spec = pl.BlockSpec((tm, head_dim), lambda j, i: (i % tiles_per_seq, 0))
        in_specs += [table_spec, table_spec]
        args += [rope["cos"], rope["sin_signed"]]
        body = functools.partial(_matmul_rope_kernel, head_dim=head_dim,
                                 key_tile0=(rope["key_col"] - col_start) // tn,
                                 key_scale=rope["key_scale"])
    return pl.pallas_call(
        body,
        grid=(n // tn, m // tm),
        in_specs=in_specs,
        out_specs=pl.BlockSpec((tm, tn), lambda j, i: (i, j)),
        out_shape=jax.ShapeDtypeStruct((m, n), out_dtype),
        scratch_shapes=scratch,
        compiler_params=_params("arbitrary", "arbitrary"),
        name=name,
    )(*args)


def _matmul_blocked_transpose_kernel(a_ref, w_ref, o_ref, wt_ref, *, dh, blk):
    @pl.when(pl.program_id(1) == 0)
    def _():
        wt_ref[...] = w_ref[...].T.astype(BF16)

    out_t = lax.dot_general(wt_ref[...], a_ref[...], _NT, preferred_element_type=F32)
    for h in range(o_ref.shape[0]):
        for b in range(o_ref.shape[1]):
            o_ref[h, b] = out_t[h * dh:(h + 1) * dh, b * blk:(b + 1) * blk].astype(o_ref.dtype)


def _matmul_blocked_transpose(a, w, layer, out_dtype, *, batch, seq, dh, blk, col_start, n_cols,
                              tm, tn, name):
    m, k = a.shape
    n = n_cols
    heads, n_blocks = n // dh, seq // blk
    tile0 = col_start // tn
    tiles_per_seq = seq // tm
    assert col_start % tn == 0 and n % tn == 0 and seq % tm == 0 and tn % dh == 0 and tm % blk == 0
    return pl.pallas_call(
        functools.partial(_matmul_blocked_transpose_kernel, dh=dh, blk=blk),
        grid=(n // tn, m // tm),
        in_specs=[pl.BlockSpec((tm, k), lambda j, i: (i, 0)),
                  pl.BlockSpec((None, k, tn), lambda j, i: (layer, 0, tile0 + j))],
        out_specs=pl.BlockSpec((None, tn // dh, tm // blk, dh, blk),
                               lambda j, i: (i // tiles_per_seq, j, i % tiles_per_seq, 0, 0)),
        out_shape=jax.ShapeDtypeStruct((batch, heads, n_blocks, dh, blk), out_dtype),
        scratch_shapes=[pltpu.VMEM((tn, k), BF16)],
        compiler_params=_params("arbitrary", "arbitrary"),
        name=name,
    )(a, w)


def _retention_kernel(lg_ref, q_ref, k_ref, v_ref, g_ref, gain_ref, o_ref,
                      state_ref, decay_ref, xi_ref, zeta_ref, *, chunk, n_chunks):
    h = pl.program_id(1)
    dv = v_ref.shape[1]
    log_gamma = lg_ref[h]

    @pl.when(pl.program_id(2) == 0)
    def _start_of_sequence():
        state_ref[...] = jnp.zeros_like(state_ref)
        r = lax.broadcasted_iota(jnp.int32, (chunk, chunk), 0)
        c = lax.broadcasted_iota(jnp.int32, (chunk, chunk), 1)
        diff = (r - c).astype(F32)
        decay_ref[...] = jnp.where(diff >= 0, jnp.exp(log_gamma * jnp.maximum(diff, 0.0)), 0.0)
        idx = lax.broadcasted_iota(jnp.int32, (chunk, dv), 0).astype(F32)
        xi_ref[...] = jnp.exp(log_gamma * (idx + 1.0))
        zeta_ref[...] = jnp.exp(log_gamma * (chunk - 1.0 - idx))

    for ci in range(n_chunks):
        rows = pl.ds(ci * chunk, chunk)
        q = q_ref[rows, :]
        k = k_ref[rows, :]
        v = v_ref[rows, :]
        scores = lax.dot_general(q, k, _NT, preferred_element_type=F32) * decay_ref[...]
        inner = jnp.dot(scores.astype(BF16), v, preferred_element_type=F32)
        state = state_ref[...]
        cross = jnp.dot(q, state.astype(BF16), preferred_element_type=F32) * xi_ref[...]
        v_decayed = (v.astype(F32) * zeta_ref[...]).astype(BF16)
        chunk_decay = xi_ref[chunk - 1:chunk, :]
        state_ref[...] = state * chunk_decay + lax.dot_general(
            k, v_decayed, _TN, preferred_element_type=F32)
        o = inner + cross
        mu = jnp.mean(o, axis=-1, keepdims=True)
        d = o - mu
        var = jnp.mean(d * d, axis=-1, keepdims=True)
        normed = d * lax.rsqrt(var + GN_EPS) * gain_ref[...]
        o_ref[rows, :] = (g_ref[rows, :].astype(F32) * normed).astype(o_ref.dtype)


def _retention_core(qk, v, gate, gn_gain, batch, seq):
    m, vwidth = v.shape
    heads = RET_HEADS
    dk = qk.shape[1] // (2 * heads)
    dv = vwidth // heads
    rows = min(RET_ROWS_PER_STEP, seq)
    chunk = RET_CHUNK
    steps = seq // rows
    assert seq % rows == 0 and rows % chunk == 0 and dk % V7X_LANES == 0 and dv % V7X_LANES == 0
    log_gamma = jnp.log1p(-jnp.power(2.0, -5.0 - jnp.arange(heads, dtype=F32)))

    row_map = lambda b, h, t: b * steps + t
    return pl.pallas_call(
        functools.partial(_retention_kernel, chunk=chunk, n_chunks=rows // chunk),
        grid=(batch, heads, steps),
        in_specs=[
            pl.BlockSpec(memory_space=pltpu.SMEM),
            pl.BlockSpec((rows, dk), lambda b, h, t: (row_map(b, h, t), h)),
            pl.BlockSpec((rows, dk), lambda b, h, t: (row_map(b, h, t), heads + h)),
            pl.BlockSpec((rows, dv), lambda b, h, t: (row_map(b, h, t), h)),
            pl.BlockSpec((rows, dv), lambda b, h, t: (row_map(b, h, t), h)),
            pl.BlockSpec((1, dv), lambda b, h, t: (0, h)),
        ],
        out_specs=pl.BlockSpec((rows, dv), lambda b, h, t: (row_map(b, h, t), h)),
        out_shape=jax.ShapeDtypeStruct((m, vwidth), BF16),
        scratch_shapes=[pltpu.VMEM((dk, dv), F32),
                        pltpu.VMEM((chunk, chunk), F32),
                        pltpu.VMEM((chunk, dv), F32),
                        pltpu.VMEM((chunk, dv), F32)],
        compiler_params=_params("arbitrary", "arbitrary", "arbitrary"),
        name="retention_core",
    )(log_gamma, qk, qk, v, gate, gn_gain.reshape(1, vwidth).astype(F32))


def _rope_tables(seq, head_dim):
    half = head_dim // 2
    inv = ROPE_BASE ** (-jnp.arange(half, dtype=F32) / half)
    ang = jnp.arange(seq).astype(F32)[:, None] * inv[None, :]
    cos = jnp.concatenate([jnp.cos(ang), jnp.cos(ang)], axis=-1)
    sin_signed = jnp.concatenate([-jnp.sin(ang), jnp.sin(ang)], axis=-1)
    return cos, sin_signed


def _t5_bucket_table(n_rel, n_buckets, max_dist):
    n = np.arange(n_rel)
    max_exact = n_buckets // 2
    nf = np.maximum(n, max_exact).astype(np.float64)
    large = max_exact + (np.log(nf / max_exact) / math.log(max_dist / max_exact)
                         * (n_buckets - max_exact)).astype(np.int64)
    large = np.minimum(large, n_buckets - 1)
    return np.where(n < max_exact, n, large).astype(np.int32)


def _moba_kernel(tbl_ref, qt_ref, qtall_ref, k_ref, vt_ref, bucket_ref, o_ref,
                 kmean_ref, bias_ref, far_bias_ref, pen_ref, s_ref, m_ref, l_ref, acc_ref,
                 *, group, n_blocks, blk, dh, topk, n_buckets, far_bucket, scale, wide, narrow):
    hg = pl.program_id(1)
    qb = pl.program_id(2)
    heads = range(group)
    seq = n_blocks * blk
    slot_prev, slot_own = n_blocks - 2, n_blocks - 1
    to_log2 = math.log2(math.e)

    def cols(g):
        return slice(g * dh, (g + 1) * dh)

    @pl.when(qb == 0)
    def _start_of_heads():
        bucket_of_rel = bucket_ref[...]
        key_pos = lax.broadcasted_iota(jnp.int32, (blk, blk), 0)
        query_pos = lax.broadcasted_iota(jnp.int32, (blk, blk), 1)
        block_id = lax.broadcasted_iota(jnp.int32, (n_blocks, blk), 0)
        for g in heads:
            for j in range(n_blocks):
                kj = k_ref[pl.ds(j * blk, blk), cols(g)].astype(F32)
                kmean_ref[g, pl.ds(j, 1), :] = jnp.mean(kj, axis=0, keepdims=True)
            kmean = kmean_ref[g]
            piece0 = kmean.astype(BF16)
            rest = kmean - piece0.astype(F32)
            piece1 = rest.astype(BF16)
            piece2 = (rest - piece1.astype(F32)).astype(BF16)
            for i in range(n_blocks):
                qt_i = qtall_ref[g, i]
                gate = (jnp.dot(piece0, qt_i, preferred_element_type=F32)
                        + jnp.dot(piece1, qt_i, preferred_element_type=F32)
                        + jnp.dot(piece2, qt_i, preferred_element_type=F32))
                past = block_id < i
                gate = jnp.where(past, gate, NEG_INF)
                chosen = jnp.zeros((n_blocks, blk), jnp.bool_)
                for _ in range(topk):
                    best = jnp.max(gate, axis=0, keepdims=True)
                    first = jnp.min(jnp.where(gate == best, block_id, n_blocks), axis=0,
                                    keepdims=True)
                    pick = block_id == first
                    chosen = chosen | pick
                    gate = jnp.where(pick, -jnp.inf, gate)
                pen_ref[g, i] = jnp.where(chosen & past, 0.0, NEG_INF)

            bias_of_rel = jnp.zeros(bucket_of_rel.shape, F32)
            for b in range(n_buckets):
                bias_of_rel = jnp.where(bucket_of_rel == b, tbl_ref[hg * group + g, b], bias_of_rel)
            bias_of_rel = bias_of_rel * to_log2
            toeplitz = pltpu.roll(jnp.broadcast_to(bias_of_rel[0:1, :], (blk, 2 * blk)), 0, 1,
                                  stride=1, stride_axis=0)
            bias_ref[g, 0] = toeplitz[:, blk:]
            bias_ref[g, 1] = jnp.where(key_pos <= query_pos, toeplitz[:, :blk], NEG_INF)
            far_bias_ref[g] = jnp.full((1, blk), tbl_ref[hg * group + g, far_bucket], F32) * to_log2

    def qk(g, block):
        kj = k_ref[pl.ds(pl.multiple_of(block * blk, blk), blk), cols(g)]
        return jnp.dot(kj, qt_ref[g], preferred_element_type=F32)

    def col_max(s):
        return jnp.max(s, axis=0, keepdims=True)

    prev_block = jnp.maximum(qb - 1, 0)
    near_dots = [(qk(g, prev_block), qk(g, qb)) for g in heads]
    for g in heads:
        d_prev, d_own = near_dots[g]
        s_prev = d_prev * (scale * to_log2) + bias_ref[g, 0] + pen_ref[g, qb, pl.ds(prev_block, 1), :]
        s_own = d_own * (scale * to_log2) + bias_ref[g, 1]
        s_ref[g, slot_prev] = s_prev
        s_ref[g, slot_own] = s_own
        m_ref[g] = jnp.maximum(col_max(s_prev), col_max(s_own))

    n_far = jnp.maximum(qb - 1, 0)
    n_wide_trips = n_far // wide
    narrow_start = n_wide_trips * wide
    n_narrow_trips = (n_far - narrow_start + narrow - 1) // narrow

    def far_blocks(start, width):
        return [(start + e, jnp.minimum(start + e, n_far - 1), start + e < n_far)
                for e in range(width)]

    def far_pass_a(blocks):
        dots = [[qk(g, block) for _, block, _ in blocks] for g in heads]
        for g in heads:
            m = m_ref[g]
            for (slot, block, real), d in zip(blocks, dots[g]):
                penalty = jnp.where(real, pen_ref[g, qb, pl.ds(block, 1), :], NEG_INF)
                s = d * (scale * to_log2) + (penalty + far_bias_ref[g])
                s_ref[g, slot] = s
                m = jnp.maximum(m, col_max(s))
            m_ref[g] = m

    def far_loops(body):
        def wide_trip(i, carry):
            body(far_blocks(i * wide, wide))
            return carry

        def narrow_trip(i, carry):
            body(far_blocks(narrow_start + i * narrow, narrow))
            return carry

        lax.fori_loop(0, n_wide_trips, wide_trip, 0)
        lax.fori_loop(0, n_narrow_trips, narrow_trip, 0)

    far_loops(far_pass_a)

    def probabilities(g, slots):
        m = m_ref[g]
        ps = [jnp.exp2(s_ref[g, slot] - m) for slot in slots]
        total = ps[0].sum(axis=0, keepdims=True)
        for p in ps[1:]:
            total = total + p.sum(axis=0, keepdims=True)
        return total, [p.astype(BF16) for p in ps]

    near = [probabilities(g, (slot_prev, slot_own)) for g in heads]
    for g in heads:
        total, (p_prev, p_own) = near[g]
        l_ref[g] = total
        acc_ref[g] = (jnp.dot(vt_ref[g, prev_block], p_prev, preferred_element_type=F32)
                      + jnp.dot(vt_ref[g, qb], p_own, preferred_element_type=F32))

    def far_pass_b(blocks):
        far = [probabilities(g, [slot for slot, _, _ in blocks]) for g in heads]
        for g in heads:
            total, ps = far[g]
            l_ref[g] = l_ref[g] + total
            acc = acc_ref[g]
            for (_, block, _), p in zip(blocks, ps):
                acc = acc + jnp.dot(vt_ref[g, block], p, preferred_element_type=F32)
            acc_ref[g] = acc

    far_loops(far_pass_b)

    for g in heads:
        o_ref[:, cols(g)] = (acc_ref[g] / l_ref[g]).T.astype(o_ref.dtype)


def _moba_core(qt, k, vt, rel_bias, batch, seq):
    m, d_model = k.shape
    n_buckets, heads = rel_bias.shape
    dh = d_model // heads
    blk = MOBA_BLOCK
    group = MOBA_HEADS_PER_STEP
    n_blocks = seq // blk
    assert seq % blk == 0 and dh % V7X_LANES == 0 and heads % group == 0
    assert blk & (blk - 1) == 0 and n_blocks >= 2

    buckets = _t5_bucket_table(max(seq, 2 * blk), n_buckets, REL_MAX_DIST)
    far_bucket = int(buckets[blk + 1])
    assert np.all(buckets[blk + 1:] == far_bucket), "blocks two or more back must share one bucket"
    bucket_of_rel = np.broadcast_to(buckets[None, :2 * blk], (V7X_SUBLANES, 2 * blk))

    gw = group * dh
    n_groups = heads // group
    return pl.pallas_call(
        functools.partial(_moba_kernel, group=group, n_blocks=n_blocks, blk=blk, dh=dh,
                          wide=MOBA_BLOCKS_PER_WIDE_TRIP, narrow=MOBA_BLOCKS_PER_NARROW_TRIP,
                          topk=min(MOBA_TOPK, n_blocks), n_buckets=n_buckets,
                          far_bucket=far_bucket, scale=dh ** -0.5),
        grid=(batch, n_groups, n_blocks),
        in_specs=[
            pl.BlockSpec(memory_space=pltpu.SMEM),
            pl.BlockSpec((None, group, None, dh, blk), lambda b, h, i: (b, h, i, 0, 0)),
            pl.BlockSpec((None, group, n_blocks, dh, blk), lambda b, h, i: (b, h, 0, 0, 0)),
            pl.BlockSpec((seq, gw), lambda b, h, i: (b, h)),
            pl.BlockSpec((None, group, n_blocks, dh, blk), lambda b, h, i: (b, h, 0, 0, 0)),
            pl.BlockSpec((V7X_SUBLANES, 2 * blk), lambda b, h, i: (0, 0)),
        ],
        out_specs=pl.BlockSpec((blk, gw), lambda b, h, i: (b * n_blocks + i, h)),
        out_shape=jax.ShapeDtypeStruct((m, d_model), BF16),
        scratch_shapes=[pltpu.VMEM((group, n_blocks, dh), F32),
                        pltpu.VMEM((group, 2, blk, blk), F32),
                        pltpu.VMEM((group, 1, blk), F32),
                        pltpu.VMEM((group, n_blocks, n_blocks, blk), F32),
                        pltpu.VMEM((group, n_blocks, blk, blk), F32),
                        pltpu.VMEM((group, 1, blk), F32),
                        pltpu.VMEM((group, 1, blk), F32),
                        pltpu.VMEM((group, dh, blk), F32)],
        compiler_params=_params("arbitrary", "arbitrary", "arbitrary"),
        name="moba_core",
    )(rel_bias.T.astype(F32), qt, qt, k, vt, jnp.asarray(bucket_of_rel))


def _ffn_up_kernel(x_ref, wg_ref, wv_ref, cwg_ref, cwv_ref, cbg_ref, cbv_ref, o_ref,
                   wgb_ref, wvb_ref, carry_ref, *, tiles_per_seq):
    _cast_weight_tile(wg_ref, wgb_ref)
    _cast_weight_tile(wv_ref, wvb_ref)

    @pl.when(pl.program_id(1) % tiles_per_seq == 0)
    def _start_of_sequence():
        carry_ref[...] = jnp.zeros_like(carry_ref)

    x = x_ref[...]
    tm = x.shape[0]
    tn = o_ref.shape[1]
    row = lax.broadcasted_iota(jnp.int32, (tm, tn), 0)

    def conv_branch(w_ref, cw_ref, cb_ref, slot):
        u = jnp.dot(x, w_ref[...], preferred_element_type=F32)
        tail = carry_ref[slot]
        prev1 = tail[V7X_SUBLANES - 1:V7X_SUBLANES, :]
        prev2 = tail[V7X_SUBLANES - 2:V7X_SUBLANES - 1, :]
        back1 = jnp.where(row == 0, prev1, pltpu.roll(u, 1, axis=0))
        back2 = jnp.where(row == 0, prev2, jnp.where(row == 1, prev1, pltpu.roll(u, 2, axis=0)))
        carry_ref[slot] = u[tm - V7X_SUBLANES:, :]
        cw = cw_ref[...]
        return cw[2:3, :] * u + cw[1:2, :] * back1 + cw[0:1, :] * back2 + cb_ref[...]

    gate = conv_branch(wgb_ref, cwg_ref, cbg_ref, 0)
    val = conv_branch(wvb_ref, cwv_ref, cbv_ref, 1)
    o_ref[...] = (gate * _sigmoid(gate) * val).astype(o_ref.dtype)


def _ffn_up(x, w_up, conv_w, conv_b, layer, seq, *, tm, tn):
    m, k = x.shape
    d_ff = w_up.shape[2] // 2
    n_col = d_ff // tn
    assert conv_w.shape[1] == CONV_WIDTH == 3 and seq % tm == 0 and d_ff % tn == 0
    return pl.pallas_call(
        functools.partial(_ffn_up_kernel, tiles_per_seq=seq // tm),
        grid=(n_col, m // tm),
        in_specs=[
            pl.BlockSpec((tm, k), lambda j, i: (i, 0)),
            pl.BlockSpec((None, k, tn), lambda j, i: (layer, 0, j)),
            pl.BlockSpec((None, k, tn), lambda j, i: (layer, 0, n_col + j)),
            pl.BlockSpec((None, CONV_WIDTH, tn), lambda j, i: (layer, 0, j)),
            pl.BlockSpec((None, CONV_WIDTH, tn), lambda j, i: (layer, 0, n_col + j)),
            pl.BlockSpec((None, 1, tn), lambda j, i: (layer, 0, j)),
            pl.BlockSpec((None, 1, tn), lambda j, i: (layer, 0, n_col + j)),
        ],
        out_specs=pl.BlockSpec((tm, tn), lambda j, i: (i, j)),
        out_shape=jax.ShapeDtypeStruct((m, d_ff), BF16),
        scratch_shapes=[pltpu.VMEM((k, tn), BF16),
                        pltpu.VMEM((k, tn), BF16),
                        pltpu.VMEM((2, V7X_SUBLANES, tn), F32)],
        compiler_params=_params("arbitrary", "arbitrary"),
        name="ffn_up_conv_gate",
    )(x, w_up, w_up, conv_w, conv_w, conv_b, conv_b)


def kernel(x, mix_norm, ret_w_in, ret_gn, ret_w_out, moba_w_qkv, moba_w_out, rel_bias,
           ffn_norm, ffn_w_up, ffn_conv_w, ffn_conv_b, ffn_w_down, final_norm):
    batch, seq, d_model = x.shape
    depth = mix_norm.shape[0]
    conv_w = ffn_conv_w.astype(F32)
    conv_b = ffn_conv_b.astype(F32)[:, None, :]
    w_ret_out = ret_w_out.astype(BF16)
    w_moba_out = moba_w_out.astype(BF16)
    w_down = ffn_w_down.astype(BF16)
    proj = _TILES["projection"]
    h = x.reshape(batch * seq, d_model)
    for i in range(depth):
        hn = _rmsnorm(h, mix_norm[i], BF16)
        j = i // N_MIXERS
        if i % N_MIXERS == 0:
            vwidth = ret_gn.shape[1]
            dk = d_model // RET_HEADS
            cos, sin_signed = _rope_tables(seq, dk)
            rope = dict(cos=cos, sin_signed=sin_signed, seq=seq, key_col=d_model,
                        key_scale=dk ** -0.5)
            qk = _matmul(hn, ret_w_in, j, BF16, name="ret_qk_proj", n_cols=2 * d_model,
                         rope=rope, **proj)
            v = _matmul(hn, ret_w_in, j, BF16, name="ret_v_proj", col_start=2 * d_model,
                        n_cols=vwidth, **proj)
            gate = _matmul(hn, ret_w_in, j, BF16, name="ret_gate_proj",
                           col_start=2 * d_model + vwidth, n_cols=vwidth, silu=True, **proj)
            y = _retention_core(qk, v, gate, ret_gn[j], batch, seq)
            h = _matmul(y, w_ret_out, j, F32, h, name="ret_out_proj", **_TILES["ret_out_proj"])
        else:
            blocked = dict(batch=batch, seq=seq, dh=d_model // rel_bias.shape[1], blk=MOBA_BLOCK,
                           n_cols=d_model, **proj)
            qt = _matmul_blocked_transpose(hn, moba_w_qkv, j, BF16, col_start=0,
                                           name="moba_qt_proj", **blocked)
            k = _matmul(hn, moba_w_qkv, j, BF16, name="moba_k_proj", col_start=d_model,
                        n_cols=d_model, **proj)
            vt = _matmul_blocked_transpose(hn, moba_w_qkv, j, BF16, col_start=2 * d_model,
                                           name="moba_vt_proj", **blocked)
            o = _moba_core(qt, k, vt, rel_bias, batch, seq)
            h = _matmul(o, w_moba_out, j, F32, h, name="moba_out_proj",
                        **_TILES["moba_out_proj"])
        hn = _rmsnorm(h, ffn_norm[i], BF16)
        a = _ffn_up(hn, ffn_w_up, conv_w, conv_b, i, seq, **_TILES["ffn_up"])
        h = _matmul(a, w_down, i, F32, h, name="ffn_down_proj", **_TILES["ffn_down_proj"])
    return _rmsnorm(h, final_norm, F32).reshape(batch, seq, d_model)
```

```python
import functools
import math

import numpy as np
import jax
import jax.numpy as jnp
from jax import lax
from jax.experimental import pallas as pl
from jax.experimental.pallas import tpu as pltpu

F32 = jnp.float32
BF16 = jnp.bfloat16

N_MIXERS = 2
RET_HEADS = 8
ROPE_BASE = 10000.0
MOBA_BLOCK = 256
MOBA_TOPK = 3
REL_MAX_DIST = 128
CONV_WIDTH = 3
RMS_EPS = 1e-6
GN_EPS = 1e-5
NEG_INF = -1e30

RET_CHUNK = 256
RET_ROWS_PER_STEP = 4096
MOBA_HEADS_PER_STEP = 4
MOBA_BLOCKS_PER_TRIP = (8, 4, 2)

V7X_SUBLANES = 8
V7X_LANES = 128
V7X_VMEM_LIMIT_BYTES = 56 * 1024 * 1024

_TILES = {
    "projection": dict(tm=1024, tn=1024),
    "ret_out_proj": dict(tm=512, tn=1024),
    "moba_out_proj": dict(tm=512, tn=2048),
    "ffn_up": dict(tm=1024, tn=512),
    "ffn_down_proj": dict(tm=512, tn=1024),
}

_NT = (((1,), (1,)), ((), ()))
_TN = (((0,), (0,)), ((), ()))


def _params(*semantics):
    return pltpu.CompilerParams(dimension_semantics=semantics,
                                vmem_limit_bytes=V7X_VMEM_LIMIT_BYTES)


def _sigmoid(x):
    return 1.0 / (1.0 + jnp.exp(-x))


def _rmsnorm_kernel(x_ref, g_ref, o_ref):
    x = x_ref[...]
    ms = jnp.mean(x * x, axis=-1, keepdims=True)
    o_ref[...] = (x * lax.rsqrt(ms + RMS_EPS) * g_ref[...]).astype(o_ref.dtype)


def _rmsnorm(x, g, out_dtype, tm=1024):
    m, d = x.shape
    return pl.pallas_call(
        _rmsnorm_kernel,
        grid=(m // tm,),
        in_specs=[pl.BlockSpec((tm, d), lambda i: (i, 0)),
                  pl.BlockSpec((1, d), lambda i: (0, 0))],
        out_specs=pl.BlockSpec((tm, d), lambda i: (i, 0)),
        out_shape=jax.ShapeDtypeStruct((m, d), out_dtype),
        compiler_params=_params("arbitrary"),
        name="rmsnorm",
    )(x, g.reshape(1, d).astype(F32))


def _cast_weight_tile(w_ref, wb_ref):
    @pl.when(pl.program_id(1) == 0)
    def _():
        wb_ref[...] = w_ref[...].astype(BF16)


def _matmul_kernel(a_ref, w_ref, o_ref, wb_ref):
    _cast_weight_tile(w_ref, wb_ref)
    o_ref[...] = jnp.dot(a_ref[...], wb_ref[...],
                         preferred_element_type=F32).astype(o_ref.dtype)


def _matmul_residual_kernel(a_ref, w_ref, r_ref, o_ref, wb_ref):
    _cast_weight_tile(w_ref, wb_ref)
    o_ref[...] = (r_ref[...] + jnp.dot(a_ref[...], wb_ref[...],
                                       preferred_element_type=F32)).astype(o_ref.dtype)


def _matmul_residual_bf16_weight_kernel(a_ref, w_ref, r_ref, o_ref):
    o_ref[...] = (r_ref[...] + jnp.dot(a_ref[...], w_ref[...],
                                       preferred_element_type=F32)).astype(o_ref.dtype)


def _matmul_silu_kernel(a_ref, w_ref, o_ref, wb_ref):
    _cast_weight_tile(w_ref, wb_ref)
    x = jnp.dot(a_ref[...], wb_ref[...], preferred_element_type=F32)
    o_ref[...] = (x * _sigmoid(x)).astype(o_ref.dtype)


def _matmul_rope_kernel(a_ref, w_ref, cos_ref, sin_ref, o_ref, wb_ref, *, head_dim, key_tile0,
                        key_scale):
    _cast_weight_tile(w_ref, wb_ref)
    a = a_ref[...]
    half = head_dim // 2
    scale = jnp.where(pl.program_id(0) >= key_tile0, key_scale, 1.0)
    for h in range(o_ref.shape[1] // head_dim):
        cols = slice(h * head_dim, (h + 1) * head_dim)
        x = jnp.dot(a, wb_ref[:, cols], preferred_element_type=F32)
        swapped = jnp.concatenate([x[:, half:], x[:, :half]], axis=1)
        rotated = x * cos_ref[...] + swapped * sin_ref[...]
        o_ref[:, cols] = (rotated * scale).astype(o_ref.dtype)


def _matmul(a, w, layer, out_dtype, residual=None, *, tm, tn, name, col_start=0, n_cols=None,
            silu=False, rope=None):
    m, k = a.shape
    n = w.shape[2] - col_start if n_cols is None else n_cols
    tile0 = col_start // tn
    assert col_start % tn == 0 and n % tn == 0 and m % tm == 0
    in_specs = [pl.BlockSpec((tm, k), lambda j, i: (i, 0)),
                pl.BlockSpec((None, k, tn), lambda j, i: (layer, 0, tile0 + j))]
    args = [a, w]
    body = _matmul_kernel
    scratch = [pltpu.VMEM((k, tn), BF16)]
    if residual is not None:
        in_specs.append(pl.BlockSpec((tm, tn), lambda j, i: (i, j)))
        args.append(residual)
        body = _matmul_residual_kernel
        if w.dtype == BF16:
            body, scratch = _matmul_residual_bf16_weight_kernel, []
    elif silu:
        body = _matmul_silu_kernel
    elif rope is not None:
        head_dim = rope["cos"].shape[1]
        tiles_per_seq = rope["seq"] // tm
        assert rope["seq"] % tm == 0 and tn % head_dim == 0 and rope["key_col"] % tn == 0
        table_spec = pl.BlockSpec((tm, head_dim), lambda j, i: (i % tiles_per_seq, 0))
        in_specs += [table_spec, table_spec]
        args += [rope["cos"], rope["sin_signed"]]
        body = functools.partial(_matmul_rope_kernel, head_dim=head_dim,
                                 key_tile0=(rope["key_col"] - col_start) // tn,
                                 key_scale=rope["key_scale"])
    return pl.pallas_call(
        body,
        grid=(n // tn, m // tm),
        in_specs=in_specs,
        out_specs=pl.BlockSpec((tm, tn), lambda j, i: (i, j)),
        out_shape=jax.ShapeDtypeStruct((m, n), out_dtype),
        scratch_shapes=scratch,
        compiler_params=_params("arbitrary", "arbitrary"),
        name=name,
    )(*args)


def _matmul_blocked_transpose_kernel(a_ref, w_ref, o_ref, wt_ref, *, dh, blk):
    @pl.when(pl.program_id(1) == 0)
    def _():
        wt_ref[...] = w_ref[...].T.astype(BF16)

    out_t = lax.dot_general(wt_ref[...], a_ref[...], _NT, preferred_element_type=F32)
    for h in range(o_ref.shape[0]):
        for b in range(o_ref.shape[1]):
            o_ref[h, b] = out_t[h * dh:(h + 1) * dh, b * blk:(b + 1) * blk].astype(o_ref.dtype)


def _matmul_blocked_transpose(a, w, layer, out_dtype, *, batch, seq, dh, blk, col_start, n_cols,
                              tm, tn, name):
    m, k = a.shape
    n = n_cols
    heads, n_blocks = n // dh, seq // blk
    tile0 = col_start // tn
    tiles_per_seq = seq // tm
    assert col_start % tn == 0 and n % tn == 0 and seq % tm == 0 and tn % dh == 0 and tm % blk == 0
    return pl.pallas_call(
        functools.partial(_matmul_blocked_transpose_kernel, dh=dh, blk=blk),
        grid=(n // tn, m // tm),
        in_specs=[pl.BlockSpec((tm, k), lambda j, i: (i, 0)),
                  pl.BlockSpec((None, k, tn), lambda j, i: (layer, 0, tile0 + j))],
        out_specs=pl.BlockSpec((None, tn // dh, tm // blk, dh, blk),
                               lambda j, i: (i // tiles_per_seq, j, i % tiles_per_seq, 0, 0)),
        out_shape=jax.ShapeDtypeStruct((batch, heads, n_blocks, dh, blk), out_dtype),
        scratch_shapes=[pltpu.VMEM((tn, k), BF16)],
        compiler_params=_params("arbitrary", "arbitrary"),
        name=name,
    )(a, w)


def _retention_kernel(lg_ref, q_ref, k_ref, v_ref, g_ref, gain_ref, o_ref,
                      state_ref, decay_ref, xi_ref, zeta_ref, *, chunk, n_chunks):
    h = pl.program_id(1)
    dv = v_ref.shape[1]
    log_gamma = lg_ref[h]

    @pl.when(pl.program_id(2) == 0)
    def _start_of_sequence():
        state_ref[...] = jnp.zeros_like(state_ref)
        r = lax.broadcasted_iota(jnp.int32, (chunk, chunk), 0)
        c = lax.broadcasted_iota(jnp.int32, (chunk, chunk), 1)
        diff = (r - c).astype(F32)
        decay_ref[...] = jnp.where(diff >= 0, jnp.exp(log_gamma * jnp.maximum(diff, 0.0)), 0.0)
        idx = lax.broadcasted_iota(jnp.int32, (chunk, dv), 0).astype(F32)
        xi_ref[...] = jnp.exp(log_gamma * (idx + 1.0))
        zeta_ref[...] = jnp.exp(log_gamma * (chunk - 1.0 - idx))

    for ci in range(n_chunks):
        rows = pl.ds(ci * chunk, chunk)
        q = q_ref[rows, :]
        k = k_ref[rows, :]
        v = v_ref[rows, :]
        scores = lax.dot_general(q, k, _NT, preferred_element_type=F32) * decay_ref[...]
        inner = jnp.dot(scores.astype(BF16), v, preferred_element_type=F32)
        state = state_ref[...]
        cross = jnp.dot(q, state.astype(BF16), preferred_element_type=F32) * xi_ref[...]
        v_decayed = (v.astype(F32) * zeta_ref[...]).astype(BF16)
        chunk_decay = xi_ref[chunk - 1:chunk, :]
        state_ref[...] = state * chunk_decay + lax.dot_general(
            k, v_decayed, _TN, preferred_element_type=F32)
        o = inner + cross
        mu = jnp.mean(o, axis=-1, keepdims=True)
        d = o - mu
        var = jnp.mean(d * d, axis=-1, keepdims=True)
        normed = d * lax.rsqrt(var + GN_EPS) * gain_ref[...]
        o_ref[rows, :] = (g_ref[rows, :].astype(F32) * normed).astype(o_ref.dtype)


def _retention_core(qk, v, gate, gn_gain, batch, seq):
    m, vwidth = v.shape
    heads = RET_HEADS
    dk = qk.shape[1] // (2 * heads)
    dv = vwidth // heads
    rows = min(RET_ROWS_PER_STEP, seq)
    chunk = RET_CHUNK
    steps = seq // rows
    assert seq % rows == 0 and rows % chunk == 0 and dk % V7X_LANES == 0 and dv % V7X_LANES == 0
    log_gamma = jnp.log1p(-jnp.power(2.0, -5.0 - jnp.arange(heads, dtype=F32)))

    row_map = lambda b, h, t: b * steps + t
    return pl.pallas_call(
        functools.partial(_retention_kernel, chunk=chunk, n_chunks=rows // chunk),
        grid=(batch, heads, steps),
        in_specs=[
            pl.BlockSpec(memory_space=pltpu.SMEM),
            pl.BlockSpec((rows, dk), lambda b, h, t: (row_map(b, h, t), h)),
            pl.BlockSpec((rows, dk), lambda b, h, t: (row_map(b, h, t), heads + h)),
            pl.BlockSpec((rows, dv), lambda b, h, t: (row_map(b, h, t), h)),
            pl.BlockSpec((rows, dv), lambda b, h, t: (row_map(b, h, t), h)),
            pl.BlockSpec((1, dv), lambda b, h, t: (0, h)),
        ],
        out_specs=pl.BlockSpec((rows, dv), lambda b, h, t: (row_map(b, h, t), h)),
        out_shape=jax.ShapeDtypeStruct((m, vwidth), BF16),
        scratch_shapes=[pltpu.VMEM((dk, dv), F32),
                        pltpu.VMEM((chunk, chunk), F32),
                        pltpu.VMEM((chunk, dv), F32),
                        pltpu.VMEM((chunk, dv), F32)],
        compiler_params=_params("arbitrary", "arbitrary", "arbitrary"),
        name="retention_core",
    )(log_gamma, qk, qk, v, gate, gn_gain.reshape(1, vwidth).astype(F32))


def _rope_tables(seq, head_dim):
    half = head_dim // 2
    inv = ROPE_BASE ** (-jnp.arange(half, dtype=F32) / half)
    ang = jnp.arange(seq).astype(F32)[:, None] * inv[None, :]
    cos = jnp.concatenate([jnp.cos(ang), jnp.cos(ang)], axis=-1)
    sin_signed = jnp.concatenate([-jnp.sin(ang), jnp.sin(ang)], axis=-1)
    return cos, sin_signed


def _t5_bucket_table(n_rel, n_buckets, max_dist):
    n = np.arange(n_rel)
    max_exact = n_buckets // 2
    nf = np.maximum(n, max_exact).astype(np.float64)
    large = max_exact + (np.log(nf / max_exact) / math.log(max_dist / max_exact)
                         * (n_buckets - max_exact)).astype(np.int64)
    large = np.minimum(large, n_buckets - 1)
    return np.where(n < max_exact, n, large).astype(np.int32)


def _moba_kernel(tbl_ref, qt_ref, qtall_ref, k_ref, vt_ref, bucket_ref, o_ref,
                 kmean_ref, bias_ref, far_bias_ref, pen_ref, s_ref, m_ref, l_ref, acc_ref,
                 *, group, n_blocks, blk, dh, topk, n_buckets, far_bucket, scale, trip_widths):
    hg = pl.program_id(1)
    qb = pl.program_id(2)
    heads = range(group)
    seq = n_blocks * blk
    slot_prev, slot_own = n_blocks - 2, n_blocks - 1
    to_log2 = math.log2(math.e)

    def cols(g):
        return slice(g * dh, (g + 1) * dh)

    @pl.when(qb == 0)
    def _start_of_heads():
        bucket_of_rel = bucket_ref[...]
        key_pos = lax.broadcasted_iota(jnp.int32, (blk, blk), 0)
        query_pos = lax.broadcasted_iota(jnp.int32, (blk, blk), 1)
        block_id = lax.broadcasted_iota(jnp.int32, (n_blocks, blk), 0)
        for g in heads:
            for j in range(n_blocks):
                kj = k_ref[pl.ds(j * blk, blk), cols(g)].astype(F32)
                kmean_ref[g, pl.ds(j, 1), :] = jnp.mean(kj, axis=0, keepdims=True)
            kmean = kmean_ref[g]
            piece0 = kmean.astype(BF16)
            rest = kmean - piece0.astype(F32)
            piece1 = rest.astype(BF16)
            piece2 = (rest - piece1.astype(F32)).astype(BF16)
            for i in range(n_blocks):
                qt_i = qtall_ref[g, i]
                gate = (jnp.dot(piece0, qt_i, preferred_element_type=F32)
                        + jnp.dot(piece1, qt_i, preferred_element_type=F32)
                        + jnp.dot(piece2, qt_i, preferred_element_type=F32))
                past = block_id < i
                gate = jnp.where(past, gate, NEG_INF)
                chosen = jnp.zeros((n_blocks, blk), jnp.bool_)
                for _ in range(topk):
                    best = jnp.max(gate, axis=0, keepdims=True)
                    first = jnp.min(jnp.where(gate == best, block_id, n_blocks), axis=0,
                                    keepdims=True)
                    pick = block_id == first
                    chosen = chosen | pick
                    gate = jnp.where(pick, -jnp.inf, gate)
                pen_ref[g, i] = jnp.where(chosen & past, 0.0, NEG_INF)

            bias_of_rel = jnp.zeros(bucket_of_rel.shape, F32)
            for b in range(n_buckets):
                bias_of_rel = jnp.where(bucket_of_rel == b, tbl_ref[hg * group + g, b], bias_of_rel)
            bias_of_rel = bias_of_rel * to_log2
            toeplitz = pltpu.roll(jnp.broadcast_to(bias_of_rel[0:1, :], (blk, 2 * blk)), 0, 1,
                                  stride=1, stride_axis=0)
            bias_ref[g, 0] = toeplitz[:, blk:]
            bias_ref[g, 1] = jnp.where(key_pos <= query_pos, toeplitz[:, :blk], NEG_INF)
            far_bias_ref[g] = jnp.full((1, blk), tbl_ref[hg * group + g, far_bucket], F32) * to_log2

    def qk(g, block):
        kj = k_ref[pl.ds(pl.multiple_of(block * blk, blk), blk), cols(g)]
        return jnp.dot(kj, qt_ref[g], preferred_element_type=F32)

    def col_max(s):
        return jnp.max(s, axis=0, keepdims=True)

    prev_block = jnp.maximum(qb - 1, 0)
    near_dots = [(qk(g, prev_block), qk(g, qb)) for g in heads]
    for g in heads:
        d_prev, d_own = near_dots[g]
        s_prev = d_prev * (scale * to_log2) + bias_ref[g, 0] + pen_ref[g, qb, pl.ds(prev_block, 1), :]
        s_own = d_own * (scale * to_log2) + bias_ref[g, 1]
        s_ref[g, slot_prev] = s_prev
        s_ref[g, slot_own] = s_own
        m_ref[g] = jnp.maximum(col_max(s_prev), col_max(s_own))

    n_far = jnp.maximum(qb - 1, 0)
    trip_plan = []
    start = 0
    for width in trip_widths[:-1]:
        n_trips = (n_far - start) // width
        trip_plan.append((width, start, n_trips))
        start = start + n_trips * width
    last = trip_widths[-1]
    trip_plan.append((last, start, (n_far - start + last - 1) // last))

    def far_blocks(start, width):
        return [(start + e, jnp.minimum(start + e, n_far - 1), start + e < n_far)
                for e in range(width)]

    def far_pass_a(blocks):
        dots = [[qk(g, block) for _, block, _ in blocks] for g in heads]
        for g in heads:
            m = m_ref[g]
            for (slot, block, real), d in zip(blocks, dots[g]):
                penalty = jnp.where(real, pen_ref[g, qb, pl.ds(block, 1), :], NEG_INF)
                s = d * (scale * to_log2) + (penalty + far_bias_ref[g])
                s_ref[g, slot] = s
                m = jnp.maximum(m, col_max(s))
            m_ref[g] = m

    def far_loops(body):
        for width, first, n_trips in trip_plan:
            def trip(i, carry, width=width, first=first):
                body(far_blocks(first + i * width, width))
                return carry

            lax.fori_loop(0, n_trips, trip, 0)

    far_loops(far_pass_a)

    def probabilities(g, slots):
        m = m_ref[g]
        ps = [jnp.exp2(s_ref[g, slot] - m) for slot in slots]
        total = ps[0].sum(axis=0, keepdims=True)
        for p in ps[1:]:
            total = total + p.sum(axis=0, keepdims=True)
        return total, [p.astype(BF16) for p in ps]

    near = [probabilities(g, (slot_prev, slot_own)) for g in heads]
    for g in heads:
        total, (p_prev, p_own) = near[g]
        l_ref[g] = total
        acc_ref[g] = (jnp.dot(vt_ref[g, prev_block], p_prev, preferred_element_type=F32)
                      + jnp.dot(vt_ref[g, qb], p_own, preferred_element_type=F32))

    def far_pass_b(blocks):
        far = [probabilities(g, [slot for slot, _, _ in blocks]) for g in heads]
        for g in heads:
            total, ps = far[g]
            l_ref[g] = l_ref[g] + total
            acc = acc_ref[g]
            for (_, block, _), p in zip(blocks, ps):
                acc = acc + jnp.dot(vt_ref[g, block], p, preferred_element_type=F32)
            acc_ref[g] = acc

    far_loops(far_pass_b)

    for g in heads:
        o_ref[:, cols(g)] = (acc_ref[g] / l_ref[g]).T.astype(o_ref.dtype)


def _moba_core(qt, k, vt, rel_bias, batch, seq):
    m, d_model = k.shape
    n_buckets, heads = rel_bias.shape
    dh = d_model // heads
    blk = MOBA_BLOCK
    group = MOBA_HEADS_PER_STEP
    n_blocks = seq // blk
    assert seq % blk == 0 and dh % V7X_LANES == 0 and heads % group == 0
    assert blk & (blk - 1) == 0 and n_blocks >= 2

    buckets = _t5_bucket_table(max(seq, 2 * blk), n_buckets, REL_MAX_DIST)
    far_bucket = int(buckets[blk + 1])
    assert np.all(buckets[blk + 1:] == far_bucket), "blocks two or more back must share one bucket"
    bucket_of_rel = np.broadcast_to(buckets[None, :2 * blk], (V7X_SUBLANES, 2 * blk))

    gw = group * dh
    n_groups = heads // group
    return pl.pallas_call(
        functools.partial(_moba_kernel, group=group, n_blocks=n_blocks, blk=blk, dh=dh,
                          trip_widths=MOBA_BLOCKS_PER_TRIP,
                          topk=min(MOBA_TOPK, n_blocks), n_buckets=n_buckets,
                          far_bucket=far_bucket, scale=dh ** -0.5),
        grid=(batch, n_groups, n_blocks),
        in_specs=[
            pl.BlockSpec(memory_space=pltpu.SMEM),
            pl.BlockSpec((None, group, None, dh, blk), lambda b, h, i: (b, h, i, 0, 0)),
            pl.BlockSpec((None, group, n_blocks, dh, blk), lambda b, h, i: (b, h, 0, 0, 0)),
            pl.BlockSpec((seq, gw), lambda b, h, i: (b, h)),
            pl.BlockSpec((None, group, n_blocks, dh, blk), lambda b, h, i: (b, h, 0, 0, 0)),
            pl.BlockSpec((V7X_SUBLANES, 2 * blk), lambda b, h, i: (0, 0)),
        ],
        out_specs=pl.BlockSpec((blk, gw), lambda b, h, i: (b * n_blocks + i, h)),
        out_shape=jax.ShapeDtypeStruct((m, d_model), BF16),
        scratch_shapes=[pltpu.VMEM((group, n_blocks, dh), F32),
                        pltpu.VMEM((group, 2, blk, blk), F32),
                        pltpu.VMEM((group, 1, blk), F32),
                        pltpu.VMEM((group, n_blocks, n_blocks, blk), F32),
                        pltpu.VMEM((group, n_blocks, blk, blk), F32),
                        pltpu.VMEM((group, 1, blk), F32),
                        pltpu.VMEM((group, 1, blk), F32),
                        pltpu.VMEM((group, dh, blk), F32)],
        compiler_params=_params("arbitrary", "arbitrary", "arbitrary"),
        name="moba_core",
    )(rel_bias.T.astype(F32), qt, qt, k, vt, jnp.asarray(bucket_of_rel))


def _ffn_up_kernel(x_ref, wg_ref, wv_ref, cwg_ref, cwv_ref, cbg_ref, cbv_ref, o_ref,
                   wgb_ref, wvb_ref, carry_ref, *, tiles_per_seq):
    _cast_weight_tile(wg_ref, wgb_ref)
    _cast_weight_tile(wv_ref, wvb_ref)

    @pl.when(pl.program_id(1) % tiles_per_seq == 0)
    def _start_of_sequence():
        carry_ref[...] = jnp.zeros_like(carry_ref)

    x = x_ref[...]
    tm = x.shape[0]
    tn = o_ref.shape[1]
    row = lax.broadcasted_iota(jnp.int32, (tm, tn), 0)

    def conv_branch(w_ref, cw_ref, cb_ref, slot):
        u = jnp.dot(x, w_ref[...], preferred_element_type=F32)
        tail = carry_ref[slot]
        prev1 = tail[V7X_SUBLANES - 1:V7X_SUBLANES, :]
        prev2 = tail[V7X_SUBLANES - 2:V7X_SUBLANES - 1, :]
        back1 = jnp.where(row == 0, prev1, pltpu.roll(u, 1, axis=0))
        back2 = jnp.where(row == 0, prev2, jnp.where(row == 1, prev1, pltpu.roll(u, 2, axis=0)))
        carry_ref[slot] = u[tm - V7X_SUBLANES:, :]
        cw = cw_ref[...]
        return cw[2:3, :] * u + cw[1:2, :] * back1 + cw[0:1, :] * back2 + cb_ref[...]

    gate = conv_branch(wgb_ref, cwg_ref, cbg_ref, 0)
    val = conv_branch(wvb_ref, cwv_ref, cbv_ref, 1)
    o_ref[...] = (gate * _sigmoid(gate) * val).astype(o_ref.dtype)


def _ffn_up(x, w_up, conv_w, conv_b, layer, seq, *, tm, tn):
    m, k = x.shape
    d_ff = w_up.shape[2] // 2
    n_col = d_ff // tn
    assert conv_w.shape[1] == CONV_WIDTH == 3 and seq % tm == 0 and d_ff % tn == 0
    return pl.pallas_call(
        functools.partial(_ffn_up_kernel, tiles_per_seq=seq // tm),
        grid=(n_col, m // tm),
        in_specs=[
            pl.BlockSpec((tm, k), lambda j, i: (i, 0)),
            pl.BlockSpec((None, k, tn), lambda j, i: (layer, 0, j)),
            pl.BlockSpec((None, k, tn), lambda j, i: (layer, 0, n_col + j)),
            pl.BlockSpec((None, CONV_WIDTH, tn), lambda j, i: (layer, 0, j)),
            pl.BlockSpec((None, CONV_WIDTH, tn), lambda j, i: (layer, 0, n_col + j)),
            pl.BlockSpec((None, 1, tn), lambda j, i: (layer, 0, j)),
            pl.BlockSpec((None, 1, tn), lambda j, i: (layer, 0, n_col + j)),
        ],
        out_specs=pl.BlockSpec((tm, tn), lambda j, i: (i, j)),
        out_shape=jax.ShapeDtypeStruct((m, d_ff), BF16),
        scratch_shapes=[pltpu.VMEM((k, tn), BF16),
                        pltpu.VMEM((k, tn), BF16),
                        pltpu.VMEM((2, V7X_SUBLANES, tn), F32)],
        compiler_params=_params("arbitrary", "arbitrary"),
        name="ffn_up_conv_gate",
    )(x, w_up, w_up, conv_w, conv_w, conv_b, conv_b)


def kernel(x, mix_norm, ret_w_in, ret_gn, ret_w_out, moba_w_qkv, moba_w_out, rel_bias,
           ffn_norm, ffn_w_up, ffn_conv_w, ffn_conv_b, ffn_w_down, final_norm):
    batch, seq, d_model = x.shape
    depth = mix_norm.shape[0]
    conv_w = ffn_conv_w.astype(F32)
    conv_b = ffn_conv_b.astype(F32)[:, None, :]
    w_ret_out = ret_w_out.astype(BF16)
    w_moba_out = moba_w_out.astype(BF16)
    w_down = ffn_w_down.astype(BF16)
    proj = _TILES["projection"]
    h = x.reshape(batch * seq, d_model)
    for i in range(depth):
        hn = _rmsnorm(h, mix_norm[i], BF16)
        j = i // N_MIXERS
        if i % N_MIXERS == 0:
            vwidth = ret_gn.shape[1]
            dk = d_model // RET_HEADS
            cos, sin_signed = _rope_tables(seq, dk)
            rope = dict(cos=cos, sin_signed=sin_signed, seq=seq, key_col=d_model,
                        key_scale=dk ** -0.5)
            qk = _matmul(hn, ret_w_in, j, BF16, name="ret_qk_proj", n_cols=2 * d_model,
                         rope=rope, **proj)
            v = _matmul(hn, ret_w_in, j, BF16, name="ret_v_proj", col_start=2 * d_model,
                        n_cols=vwidth, **proj)
            gate = _matmul(hn, ret_w_in, j, BF16, name="ret_gate_proj",
                           col_start=2 * d_model + vwidth, n_cols=vwidth, silu=True, **proj)
            y = _retention_core(qk, v, gate, ret_gn[j], batch, seq)
            h = _matmul(y, w_ret_out, j, F32, h, name="ret_out_proj", **_TILES["ret_out_proj"])
        else:
            blocked = dict(batch=batch, seq=seq, dh=d_model // rel_bias.shape[1], blk=MOBA_BLOCK,
                           n_cols=d_model, **proj)
            qt = _matmul_blocked_transpose(hn, moba_w_qkv, j, BF16, col_start=0,
                                           name="moba_qt_proj", **blocked)
            k = _matmul(hn, moba_w_qkv, j, BF16, name="moba_k_proj", col_start=d_model,
                        n_cols=d_model, **proj)
            vt = _matmul_blocked_transpose(hn, moba_w_qkv, j, BF16, col_start=2 * d_model,
                                           name="moba_vt_proj", **blocked)
            o = _moba_core(qt, k, vt, rel_bias, batch, seq)
            h = _matmul(o, w_moba_out, j, F32, h, name="moba_out_proj",
                        **_TILES["moba_out_proj"])
        hn = _rmsnorm(h, ffn_norm[i], BF16)
        a = _ffn_up(hn, ffn_w_up, conv_w, conv_b, i, seq, **_TILES["ffn_up"])
        h = _matmul(a, w_down, i, F32, h, name="ffn_down_proj", **_TILES["ffn_down_proj"])
    return _rmsnorm(h, final_norm, F32).reshape(batch, seq, d_model)
```

```python
import functools
import math

import numpy as np
import jax
import jax.numpy as jnp
from jax import lax
from jax.experimental import pallas as pl
from jax.experimental.pallas import tpu as pltpu

F32 = jnp.float32
BF16 = jnp.bfloat16

N_MIXERS = 2
RET_HEADS = 8
ROPE_BASE = 10000.0
MOBA_BLOCK = 256
MOBA_TOPK = 3
REL_MAX_DIST = 128
CONV_WIDTH = 3
RMS_EPS = 1e-6
GN_EPS = 1e-5
NEG_INF = -1e30

RET_CHUNK = 256
RET_ROWS_PER_STEP = 4096
MOBA_HEADS_PER_STEP = 4
MOBA_BLOCKS_PER_TRIP = (8, 4, 2)
FFN_ROWS_PER_DOT = 1024

V7X_SUBLANES = 8
V7X_LANES = 128
V7X_VMEM_LIMIT_BYTES = 56 * 1024 * 1024

_TILES = {
    "projection": dict(tm=1024, tn=1024),
    "ret_out_proj": dict(tm=512, tn=1024),
    "moba_out_proj": dict(tm=512, tn=2048),
    "ffn_up": dict(tm=2048, tn=512),
    "ffn_down_proj": dict(tm=512, tn=1024),
}

_NT = (((1,), (1,)), ((), ()))
_TN = (((0,), (0,)), ((), ()))


def _params(*semantics):
    return pltpu.CompilerParams(dimension_semantics=semantics,
                                vmem_limit_bytes=V7X_VMEM_LIMIT_BYTES)


def _sigmoid(x):
    return 1.0 / (1.0 + jnp.exp(-x))


def _rmsnorm_kernel(x_ref, g_ref, o_ref):
    x = x_ref[...]
    ms = jnp.mean(x * x, axis=-1, keepdims=True)
    o_ref[...] = (x * lax.rsqrt(ms + RMS_EPS) * g_ref[...]).astype(o_ref.dtype)


def _rmsnorm(x, g, out_dtype, tm=1024):
    m, d = x.shape
    return pl.pallas_call(
        _rmsnorm_kernel,
        grid=(m // tm,),
        in_specs=[pl.BlockSpec((tm, d), lambda i: (i, 0)),
                  pl.BlockSpec((1, d), lambda i: (0, 0))],
        out_specs=pl.BlockSpec((tm, d), lambda i: (i, 0)),
        out_shape=jax.ShapeDtypeStruct((m, d), out_dtype),
        compiler_params=_params("arbitrary"),
        name="rmsnorm",
    )(x, g.reshape(1, d).astype(F32))


def _cast_weight_tile(w_ref, wb_ref):
    @pl.when(pl.program_id(1) == 0)
    def _():
        wb_ref[...] = w_ref[...].astype(BF16)


def _matmul_kernel(a_ref, w_ref, o_ref, wb_ref):
    _cast_weight_tile(w_ref, wb_ref)
    o_ref[...] = jnp.dot(a_ref[...], wb_ref[...],
                         preferred_element_type=F32).astype(o_ref.dtype)


def _matmul_residual_kernel(a_ref, w_ref, r_ref, o_ref, wb_ref):
    _cast_weight_tile(w_ref, wb_ref)
    o_ref[...] = (r_ref[...] + jnp.dot(a_ref[...], wb_ref[...],
                                       preferred_element_type=F32)).astype(o_ref.dtype)


def _matmul_residual_bf16_weight_kernel(a_ref, w_ref, r_ref, o_ref):
    o_ref[...] = (r_ref[...] + jnp.dot(a_ref[...], w_ref[...],
                                       preferred_element_type=F32)).astype(o_ref.dtype)


def _matmul_silu_kernel(a_ref, w_ref, o_ref, wb_ref):
    _cast_weight_tile(w_ref, wb_ref)
    x = jnp.dot(a_ref[...], wb_ref[...], preferred_element_type=F32)
    o_ref[...] = (x * _sigmoid(x)).astype(o_ref.dtype)


def _matmul_rope_kernel(a_ref, w_ref, cos_ref, sin_ref, o_ref, wb_ref, *, head_dim, key_tile0,
                        key_scale):
    _cast_weight_tile(w_ref, wb_ref)
    a = a_ref[...]
    half = head_dim // 2
    scale = jnp.where(pl.program_id(0) >= key_tile0, key_scale, 1.0)
    for h in range(o_ref.shape[1] // head_dim):
        cols = slice(h * head_dim, (h + 1) * head_dim)
        x = jnp.dot(a, wb_ref[:, cols], preferred_element_type=F32)
        swapped = jnp.concatenate([x[:, half:], x[:, :half]], axis=1)
        rotated = x * cos_ref[...] + swapped * sin_ref[...]
        o_ref[:, cols] = (rotated * scale).astype(o_ref.dtype)


def _matmul(a, w, layer, out_dtype, residual=None, *, tm, tn, name, col_start=0, n_cols=None,
            silu=False, rope=None):
    m, k = a.shape
    n = w.shape[2] - col_start if n_cols is None else n_cols
    tile0 = col_start // tn
    assert col_start % tn == 0 and n % tn == 0 and m % tm == 0
    in_specs = [pl.BlockSpec((tm, k), lambda j, i: (i, 0)),
                pl.BlockSpec((None, k, tn), lambda j, i: (layer, 0, tile0 + j))]
    args = [a, w]
    body = _matmul_kernel
    scratch = [pltpu.VMEM((k, tn), BF16)]
    if residual is not None:
        in_specs.append(pl.BlockSpec((tm, tn), lambda j, i: (i, j)))
        args.append(residual)
        body = _matmul_residual_kernel
        if w.dtype == BF16:
            body, scratch = _matmul_residual_bf16_weight_kernel, []
    elif silu:
        body = _matmul_silu_kernel
    elif rope is not None:
        head_dim = rope["cos"].shape[1]
        tiles_per_seq = rope["seq"] // tm
        assert rope["seq"] % tm == 0 and tn % head_dim == 0 and rope["key_col"] % tn == 0
        table_spec = pl.BlockSpec((tm, head_dim), lambda j, i: (i % tiles_per_seq, 0))
        in_specs += [table_spec, table_spec]
        args += [rope["cos"], rope["sin_signed"]]
        body = functools.partial(_matmul_rope_kernel, head_dim=head_dim,
                                 key_tile0=(rope["key_col"] - col_start) // tn,
                                 key_scale=rope["key_scale"])
    return pl.pallas_call(
        body,
        grid=(n // tn, m // tm),
        in_specs=in_specs,
        out_specs=pl.BlockSpec((tm, tn), lambda j, i: (i, j)),
        out_shape=jax.ShapeDtypeStruct((m, n), out_dtype),
        scratch_shapes=scratch,
        compiler_params=_params("arbitrary", "arbitrary"),
        name=name,
    )(*args)


def _matmul_blocked_transpose_kernel(a_ref, w_ref, o_ref, wt_ref, *, dh, blk):
    @pl.when(pl.program_id(1) == 0)
    def _():
        wt_ref[...] = w_ref[...].T.astype(BF16)

    out_t = lax.dot_general(wt_ref[...], a_ref[...], _NT, preferred_element_type=F32)
    for h in range(o_ref.shape[0]):
        for b in range(o_ref.shape[1]):
            o_ref[h, b] = out_t[h * dh:(h + 1) * dh, b * blk:(b + 1) * blk].astype(o_ref.dtype)


def _matmul_blocked_transpose(a, w, layer, out_dtype, *, batch, seq, dh, blk, col_start, n_cols,
                              tm, tn, name):
    m, k = a.shape
    n = n_cols
    heads, n_blocks = n // dh, seq // blk
    tile0 = col_start // tn
    tiles_per_seq = seq // tm
    assert col_start % tn == 0 and n % tn == 0 and seq % tm == 0 and tn % dh == 0 and tm % blk == 0
    return pl.pallas_call(
        functools.partial(_matmul_blocked_transpose_kernel, dh=dh, blk=blk),
        grid=(n // tn, m // tm),
        in_specs=[pl.BlockSpec((tm, k), lambda j, i: (i, 0)),
                  pl.BlockSpec((None, k, tn), lambda j, i: (layer, 0, tile0 + j))],
        out_specs=pl.BlockSpec((None, tn // dh, tm // blk, dh, blk),
                               lambda j, i: (i // tiles_per_seq, j, i % tiles_per_seq, 0, 0)),
        out_shape=jax.ShapeDtypeStruct((batch, heads, n_blocks, dh, blk), out_dtype),
        scratch_shapes=[pltpu.VMEM((tn, k), BF16)],
        compiler_params=_params("arbitrary", "arbitrary"),
        name=name,
    )(a, w)


def _retention_kernel(lg_ref, q_ref, k_ref, v_ref, g_ref, gain_ref, o_ref,
                      state_ref, decay_ref, xi_ref, zeta_ref, *, chunk, n_chunks):
    h = pl.program_id(1)
    dv = v_ref.shape[1]
    log_gamma = lg_ref[h]

    @pl.when(pl.program_id(2) == 0)
    def _start_of_sequence():
        state_ref[...] = jnp.zeros_like(state_ref)
        r = lax.broadcasted_iota(jnp.int32, (chunk, chunk), 0)
        c = lax.broadcasted_iota(jnp.int32, (chunk, chunk), 1)
        diff = (r - c).astype(F32)
        decay_ref[...] = jnp.where(diff >= 0, jnp.exp(log_gamma * jnp.maximum(diff, 0.0)), 0.0)
        idx = lax.broadcasted_iota(jnp.int32, (chunk, dv), 0).astype(F32)
        xi_ref[...] = jnp.exp(log_gamma * (idx + 1.0))
        zeta_ref[...] = jnp.exp(log_gamma * (chunk - 1.0 - idx))

    for ci in range(n_chunks):
        rows = pl.ds(ci * chunk, chunk)
        q = q_ref[rows, :]
        k = k_ref[rows, :]
        v = v_ref[rows, :]
        scores = lax.dot_general(q, k, _NT, preferred_element_type=F32) * decay_ref[...]
        inner = jnp.dot(scores.astype(BF16), v, preferred_element_type=F32)
        state = state_ref[...]
        cross = jnp.dot(q, state.astype(BF16), preferred_element_type=F32) * xi_ref[...]
        v_decayed = (v.astype(F32) * zeta_ref[...]).astype(BF16)
        chunk_decay = xi_ref[chunk - 1:chunk, :]
        state_ref[...] = state * chunk_decay + lax.dot_general(
            k, v_decayed, _TN, preferred_element_type=F32)
        o = inner + cross
        mu = jnp.mean(o, axis=-1, keepdims=True)
        d = o - mu
        var = jnp.mean(d * d, axis=-1, keepdims=True)
        normed = d * lax.rsqrt(var + GN_EPS) * gain_ref[...]
        o_ref[rows, :] = (g_ref[rows, :].astype(F32) * normed).astype(o_ref.dtype)


def _retention_core(qk, v, gate, gn_gain, batch, seq):
    m, vwidth = v.shape
    heads = RET_HEADS
    dk = qk.shape[1] // (2 * heads)
    dv = vwidth // heads
    rows = min(RET_ROWS_PER_STEP, seq)
    chunk = RET_CHUNK
    steps = seq // rows
    assert seq % rows == 0 and rows % chunk == 0 and dk % V7X_LANES == 0 and dv % V7X_LANES == 0
    log_gamma = jnp.log1p(-jnp.power(2.0, -5.0 - jnp.arange(heads, dtype=F32)))

    row_map = lambda b, h, t: b * steps + t
    return pl.pallas_call(
        functools.partial(_retention_kernel, chunk=chunk, n_chunks=rows // chunk),
        grid=(batch, heads, steps),
        in_specs=[
            pl.BlockSpec(memory_space=pltpu.SMEM),
            pl.BlockSpec((rows, dk), lambda b, h, t: (row_map(b, h, t), h)),
            pl.BlockSpec((rows, dk), lambda b, h, t: (row_map(b, h, t), heads + h)),
            pl.BlockSpec((rows, dv), lambda b, h, t: (row_map(b, h, t), h)),
            pl.BlockSpec((rows, dv), lambda b, h, t: (row_map(b, h, t), h)),
            pl.BlockSpec((1, dv), lambda b, h, t: (0, h)),
        ],
        out_specs=pl.BlockSpec((rows, dv), lambda b, h, t: (row_map(b, h, t), h)),
        out_shape=jax.ShapeDtypeStruct((m, vwidth), BF16),
        scratch_shapes=[pltpu.VMEM((dk, dv), F32),
                        pltpu.VMEM((chunk, chunk), F32),
                        pltpu.VMEM((chunk, dv), F32),
                        pltpu.VMEM((chunk, dv), F32)],
        compiler_params=_params("arbitrary", "arbitrary", "arbitrary"),
        name="retention_core",
    )(log_gamma, qk, qk, v, gate, gn_gain.reshape(1, vwidth).astype(F32))


def _rope_tables(seq, head_dim):
    half = head_dim // 2
    inv = ROPE_BASE ** (-jnp.arange(half, dtype=F32) / half)
    ang = jnp.arange(seq).astype(F32)[:, None] * inv[None, :]
    cos = jnp.concatenate([jnp.cos(ang), jnp.cos(ang)], axis=-1)
    sin_signed = jnp.concatenate([-jnp.sin(ang), jnp.sin(ang)], axis=-1)
    return cos, sin_signed


def _t5_bucket_table(n_rel, n_buckets, max_dist):
    n = np.arange(n_rel)
    max_exact = n_buckets // 2
    nf = np.maximum(n, max_exact).astype(np.float64)
    large = max_exact + (np.log(nf / max_exact) / math.log(max_dist / max_exact)
                         * (n_buckets - max_exact)).astype(np.int64)
    large = np.minimum(large, n_buckets - 1)
    return np.where(n < max_exact, n, large).astype(np.int32)


def _moba_kernel(tbl_ref, qt_ref, qtall_ref, k_ref, vt_ref, bucket_ref, o_ref,
                 kmean_ref, bias_ref, far_bias_ref, pen_ref, s_ref, m_ref, l_ref, acc_ref,
                 *, group, n_blocks, blk, dh, topk, n_buckets, far_bucket, scale, trip_widths):
    hg = pl.program_id(1)
    qb = pl.program_id(2)
    heads = range(group)
    seq = n_blocks * blk
    slot_prev, slot_own = n_blocks - 2, n_blocks - 1
    to_log2 = math.log2(math.e)

    def cols(g):
        return slice(g * dh, (g + 1) * dh)

    @pl.when(qb == 0)
    def _start_of_heads():
        bucket_of_rel = bucket_ref[...]
        key_pos = lax.broadcasted_iota(jnp.int32, (blk, blk), 0)
        query_pos = lax.broadcasted_iota(jnp.int32, (blk, blk), 1)
        block_id = lax.broadcasted_iota(jnp.int32, (n_blocks, blk), 0)
        for g in heads:
            for j in range(n_blocks):
                kj = k_ref[pl.ds(j * blk, blk), cols(g)].astype(F32)
                kmean_ref[g, pl.ds(j, 1), :] = jnp.mean(kj, axis=0, keepdims=True)
            kmean = kmean_ref[g]
            piece0 = kmean.astype(BF16)
            rest = kmean - piece0.astype(F32)
            piece1 = rest.astype(BF16)
            piece2 = (rest - piece1.astype(F32)).astype(BF16)
            for i in range(n_blocks):
                qt_i = qtall_ref[g, i]
                gate = (jnp.dot(piece0, qt_i, preferred_element_type=F32)
                        + jnp.dot(piece1, qt_i, preferred_element_type=F32)
                        + jnp.dot(piece2, qt_i, preferred_element_type=F32))
                past = block_id < i
                gate = jnp.where(past, gate, NEG_INF)
                chosen = jnp.zeros((n_blocks, blk), jnp.bool_)
                for _ in range(topk):
                    best = jnp.max(gate, axis=0, keepdims=True)
                    first = jnp.min(jnp.where(gate == best, block_id, n_blocks), axis=0,
                                    keepdims=True)
                    pick = block_id == first
                    chosen = chosen | pick
                    gate = jnp.where(pick, -jnp.inf, gate)
                pen_ref[g, i] = jnp.where(chosen & past, 0.0, NEG_INF)

            bias_of_rel = jnp.zeros(bucket_of_rel.shape, F32)
            for b in range(n_buckets):
                bias_of_rel = jnp.where(bucket_of_rel == b, tbl_ref[hg * group + g, b], bias_of_rel)
            bias_of_rel = bias_of_rel * to_log2
            toeplitz = pltpu.roll(jnp.broadcast_to(bias_of_rel[0:1, :], (blk, 2 * blk)), 0, 1,
                                  stride=1, stride_axis=0)
            bias_ref[g, 0] = toeplitz[:, blk:]
            bias_ref[g, 1] = jnp.where(key_pos <= query_pos, toeplitz[:, :blk], NEG_INF)
            far_bias_ref[g] = jnp.full((1, blk), tbl_ref[hg * group + g, far_bucket], F32) * to_log2

    def qk(g, block):
        kj = k_ref[pl.ds(pl.multiple_of(block * blk, blk), blk), cols(g)]
        return jnp.dot(kj, qt_ref[g], preferred_element_type=F32)

    def col_max(s):
        return jnp.max(s, axis=0, keepdims=True)

    prev_block = jnp.maximum(qb - 1, 0)
    near_dots = [(qk(g, prev_block), qk(g, qb)) for g in heads]
    for g in heads:
        d_prev, d_own = near_dots[g]
        s_prev = d_prev * (scale * to_log2) + bias_ref[g, 0] + pen_ref[g, qb, pl.ds(prev_block, 1), :]
        s_own = d_own * (scale * to_log2) + bias_ref[g, 1]
        s_ref[g, slot_prev] = s_prev
        s_ref[g, slot_own] = s_own
        m_ref[g] = jnp.maximum(col_max(s_prev), col_max(s_own))

    n_far = jnp.maximum(qb - 1, 0)
    trip_plan = []
    start = 0
    for width in trip_widths[:-1]:
        n_trips = (n_far - start) // width
        trip_plan.append((width, start, n_trips))
        start = start + n_trips * width
    last = trip_widths[-1]
    trip_plan.append((last, start, (n_far - start + last - 1) // last))

    def far_blocks(start, width):
        return [(start + e, jnp.minimum(start + e, n_far - 1), start + e < n_far)
                for e in range(width)]

    def far_pass_a(blocks):
        dots = [[qk(g, block) for _, block, _ in blocks] for g in heads]
        for g in heads:
            m = m_ref[g]
            for (slot, block, real), d in zip(blocks, dots[g]):
                penalty = jnp.where(real, pen_ref[g, qb, pl.ds(block, 1), :], NEG_INF)
                s = d * (scale * to_log2) + (penalty + far_bias_ref[g])
                s_ref[g, slot] = s
                m = jnp.maximum(m, col_max(s))
            m_ref[g] = m

    def far_loops(body):
        for width, first, n_trips in trip_plan:
            def trip(i, carry, width=width, first=first):
                body(far_blocks(first + i * width, width))
                return carry

            lax.fori_loop(0, n_trips, trip, 0)

    far_loops(far_pass_a)

    def probabilities(g, slots):
        m = m_ref[g]
        ps = [jnp.exp2(s_ref[g, slot] - m) for slot in slots]
        total = ps[0].sum(axis=0, keepdims=True)
        for p in ps[1:]:
            total = total + p.sum(axis=0, keepdims=True)
        return total, [p.astype(BF16) for p in ps]

    near = [probabilities(g, (slot_prev, slot_own)) for g in heads]
    for g in heads:
        total, (p_prev, p_own) = near[g]
        l_ref[g] = total
        acc_ref[g] = (jnp.dot(vt_ref[g, prev_block], p_prev, preferred_element_type=F32)
                      + jnp.dot(vt_ref[g, qb], p_own, preferred_element_type=F32))

    def far_pass_b(blocks):
        far = [probabilities(g, [slot for slot, _, _ in blocks]) for g in heads]
        for g in heads:
            total, ps = far[g]
            l_ref[g] = l_ref[g] + total
            acc = acc_ref[g]
            for (_, block, _), p in zip(blocks, ps):
                acc = acc + jnp.dot(vt_ref[g, block], p, preferred_element_type=F32)
            acc_ref[g] = acc

    far_loops(far_pass_b)

    for g in heads:
        o_ref[:, cols(g)] = (acc_ref[g] / l_ref[g]).T.astype(o_ref.dtype)


def _moba_core(qt, k, vt, rel_bias, batch, seq):
    m, d_model = k.shape
    n_buckets, heads = rel_bias.shape
    dh = d_model // heads
    blk = MOBA_BLOCK
    group = MOBA_HEADS_PER_STEP
    n_blocks = seq // blk
    assert seq % blk == 0 and dh % V7X_LANES == 0 and heads % group == 0
    assert blk & (blk - 1) == 0 and n_blocks >= 2

    buckets = _t5_bucket_table(max(seq, 2 * blk), n_buckets, REL_MAX_DIST)
    far_bucket = int(buckets[blk + 1])
    assert np.all(buckets[blk + 1:] == far_bucket), "blocks two or more back must share one bucket"
    bucket_of_rel = np.broadcast_to(buckets[None, :2 * blk], (V7X_SUBLANES, 2 * blk))

    gw = group * dh
    n_groups = heads // group
    return pl.pallas_call(
        functools.partial(_moba_kernel, group=group, n_blocks=n_blocks, blk=blk, dh=dh,
                          trip_widths=MOBA_BLOCKS_PER_TRIP,
                          topk=min(MOBA_TOPK, n_blocks), n_buckets=n_buckets,
                          far_bucket=far_bucket, scale=dh ** -0.5),
        grid=(batch, n_groups, n_blocks),
        in_specs=[
            pl.BlockSpec(memory_space=pltpu.SMEM),
            pl.BlockSpec((None, group, None, dh, blk), lambda b, h, i: (b, h, i, 0, 0)),
            pl.BlockSpec((None, group, n_blocks, dh, blk), lambda b, h, i: (b, h, 0, 0, 0)),
            pl.BlockSpec((seq, gw), lambda b, h, i: (b, h)),
            pl.BlockSpec((None, group, n_blocks, dh, blk), lambda b, h, i: (b, h, 0, 0, 0)),
            pl.BlockSpec((V7X_SUBLANES, 2 * blk), lambda b, h, i: (0, 0)),
        ],
        out_specs=pl.BlockSpec((blk, gw), lambda b, h, i: (b * n_blocks + i, h)),
        out_shape=jax.ShapeDtypeStruct((m, d_model), BF16),
        scratch_shapes=[pltpu.VMEM((group, n_blocks, dh), F32),
                        pltpu.VMEM((group, 2, blk, blk), F32),
                        pltpu.VMEM((group, 1, blk), F32),
                        pltpu.VMEM((group, n_blocks, n_blocks, blk), F32),
                        pltpu.VMEM((group, n_blocks, blk, blk), F32),
                        pltpu.VMEM((group, 1, blk), F32),
                        pltpu.VMEM((group, 1, blk), F32),
                        pltpu.VMEM((group, dh, blk), F32)],
        compiler_params=_params("arbitrary", "arbitrary", "arbitrary"),
        name="moba_core",
    )(rel_bias.T.astype(F32), qt, qt, k, vt, jnp.asarray(bucket_of_rel))


def _ffn_up_kernel(x_ref, wg_ref, wv_ref, cwg_ref, cwv_ref, cbg_ref, cbv_ref, o_ref,
                   wgb_ref, wvb_ref, carry_ref, *, tiles_per_seq):
    _cast_weight_tile(wg_ref, wgb_ref)
    _cast_weight_tile(wv_ref, wvb_ref)

    @pl.when(pl.program_id(1) % tiles_per_seq == 0)
    def _start_of_sequence():
        carry_ref[...] = jnp.zeros_like(carry_ref)

    tm, tn = o_ref.shape
    rc = min(tm, FFN_ROWS_PER_DOT)
    row = lax.broadcasted_iota(jnp.int32, (rc, tn), 0)

    def conv_branch(x, w_ref, cw_ref, cb_ref, slot):
        u = jnp.dot(x, w_ref[...], preferred_element_type=F32)
        tail = carry_ref[slot]
        prev1 = tail[V7X_SUBLANES - 1:V7X_SUBLANES, :]
        prev2 = tail[V7X_SUBLANES - 2:V7X_SUBLANES - 1, :]
        back1 = jnp.where(row == 0, prev1, pltpu.roll(u, 1, axis=0))
        back2 = jnp.where(row == 0, prev2, jnp.where(row == 1, prev1, pltpu.roll(u, 2, axis=0)))
        carry_ref[slot] = u[rc - V7X_SUBLANES:, :]
        cw = cw_ref[...]
        return cw[2:3, :] * u + cw[1:2, :] * back1 + cw[0:1, :] * back2 + cb_ref[...]

    for r in range(tm // rc):
        rows = pl.ds(r * rc, rc)
        x = x_ref[rows, :]
        gate = conv_branch(x, wgb_ref, cwg_ref, cbg_ref, 0)
        val = conv_branch(x, wvb_ref, cwv_ref, cbv_ref, 1)
        o_ref[rows, :] = (gate * _sigmoid(gate) * val).astype(o_ref.dtype)


def _ffn_up(x, w_up, conv_w, conv_b, layer, seq, *, tm, tn):
    m, k = x.shape
    d_ff = w_up.shape[2] // 2
    n_col = d_ff // tn
    assert conv_w.shape[1] == CONV_WIDTH == 3 and seq % tm == 0 and d_ff % tn == 0
    return pl.pallas_call(
        functools.partial(_ffn_up_kernel, tiles_per_seq=seq // tm),
        grid=(n_col, m // tm),
        in_specs=[
            pl.BlockSpec((tm, k), lambda j, i: (i, 0)),
            pl.BlockSpec((None, k, tn), lambda j, i: (layer, 0, j)),
            pl.BlockSpec((None, k, tn), lambda j, i: (layer, 0, n_col + j)),
            pl.BlockSpec((None, CONV_WIDTH, tn), lambda j, i: (layer, 0, j)),
            pl.BlockSpec((None, CONV_WIDTH, tn), lambda j, i: (layer, 0, n_col + j)),
            pl.BlockSpec((None, 1, tn), lambda j, i: (layer, 0, j)),
            pl.BlockSpec((None, 1, tn), lambda j, i: (layer, 0, n_col + j)),
        ],
        out_specs=pl.BlockSpec((tm, tn), lambda j, i: (i, j)),
        out_shape=jax.ShapeDtypeStruct((m, d_ff), BF16),
        scratch_shapes=[pltpu.VMEM((k, tn), BF16),
                        pltpu.VMEM((k, tn), BF16),
                        pltpu.VMEM((2, V7X_SUBLANES, tn), F32)],
        compiler_params=_params("arbitrary", "arbitrary"),
        name="ffn_up_conv_gate",
    )(x, w_up, w_up, conv_w, conv_w, conv_b, conv_b)


def kernel(x, mix_norm, ret_w_in, ret_gn, ret_w_out, moba_w_qkv, moba_w_out, rel_bias,
           ffn_norm, ffn_w_up, ffn_conv_w, ffn_conv_b, ffn_w_down, final_norm):
    batch, seq, d_model = x.shape
    depth = mix_norm.shape[0]
    conv_w = ffn_conv_w.astype(F32)
    conv_b = ffn_conv_b.astype(F32)[:, None, :]
    w_ret_out = ret_w_out.astype(BF16)
    w_moba_out = moba_w_out.astype(BF16)
    w_down = ffn_w_down.astype(BF16)
    proj = _TILES["projection"]
    h = x.reshape(batch * seq, d_model)
    for i in range(depth):
        hn = _rmsnorm(h, mix_norm[i], BF16)
        j = i // N_MIXERS
        if i % N_MIXERS == 0:
            vwidth = ret_gn.shape[1]
            dk = d_model // RET_HEADS
            cos, sin_signed = _rope_tables(seq, dk)
            rope = dict(cos=cos, sin_signed=sin_signed, seq=seq, key_col=d_model,
                        key_scale=dk ** -0.5)
            qk = _matmul(hn, ret_w_in, j, BF16, name="ret_qk_proj", n_cols=2 * d_model,
                         rope=rope, **proj)
            v = _matmul(hn, ret_w_in, j, BF16, name="ret_v_proj", col_start=2 * d_model,
                        n_cols=vwidth, **proj)
            gate = _matmul(hn, ret_w_in, j, BF16, name="ret_gate_proj",
                           col_start=2 * d_model + vwidth, n_cols=vwidth, silu=True, **proj)
            y = _retention_core(qk, v, gate, ret_gn[j], batch, seq)
            h = _matmul(y, w_ret_out, j, F32, h, name="ret_out_proj", **_TILES["ret_out_proj"])
        else:
            blocked = dict(batch=batch, seq=seq, dh=d_model // rel_bias.shape[1], blk=MOBA_BLOCK,
                           n_cols=d_model, **proj)
            qt = _matmul_blocked_transpose(hn, moba_w_qkv, j, BF16, col_start=0,
                                           name="moba_qt_proj", **blocked)
            k = _matmul(hn, moba_w_qkv, j, BF16, name="moba_k_proj", col_start=d_model,
                        n_cols=d_model, **proj)
            vt = _matmul_blocked_transpose(hn, moba_w_qkv, j, BF16, col_start=2 * d_model,
                                           name="moba_vt_proj", **blocked)
            o = _moba_core(qt, k, vt, rel_bias, batch, seq)
            h = _matmul(o, w_moba_out, j, F32, h, name="moba_out_proj",
                        **_TILES["moba_out_proj"])
        hn = _rmsnorm(h, ffn_norm[i], BF16)
        a = _ffn_up(hn, ffn_w_up, conv_w, conv_b, i, seq, **_TILES["ffn_up"])
        h = _matmul(a, w_down, i, F32, h, name="ffn_down_proj", **_TILES["ffn_down_proj"])
    return _rmsnorm(h, final_norm, F32).reshape(batch, seq, d_model)
```

```python
import functools
import math

import numpy as np
import jax
import jax.numpy as jnp
from jax import lax
from jax.experimental import pallas as pl
from jax.experimental.pallas import tpu as pltpu

F32 = jnp.float32
BF16 = jnp.bfloat16

N_MIXERS = 2
RET_HEADS = 8
ROPE_BASE = 10000.0
MOBA_BLOCK = 256
MOBA_TOPK = 3
REL_MAX_DIST = 128
CONV_WIDTH = 3
RMS_EPS = 1e-6
GN_EPS = 1e-5
NEG_INF = -1e30

RET_CHUNK = 256
RET_ROWS_PER_STEP = 4096
MOBA_HEADS_PER_STEP = 4
MOBA_BLOCKS_PER_TRIP = (8, 4, 2)

V7X_SUBLANES = 8
V7X_LANES = 128
V7X_VMEM_LIMIT_BYTES = 56 * 1024 * 1024

_TILES = {
    "projection": dict(tm=1024, tn=1024),
    "ret_out_proj": dict(tm=512, tn=1024),
    "moba_out_proj": dict(tm=512, tn=2048),
    "ffn_up": dict(tm=1024, tn=512),
    "ffn_down_proj": dict(tm=512, tn=1024),
}

_NT = (((1,), (1,)), ((), ()))
_TN = (((0,), (0,)), ((), ()))


def _params(*semantics):
    return pltpu.CompilerParams(dimension_semantics=semantics,
                                vmem_limit_bytes=V7X_VMEM_LIMIT_BYTES)


def _sigmoid(x):
    return 1.0 / (1.0 + jnp.exp(-x))


def _rmsnorm_kernel(x_ref, g_ref, o_ref):
    x = x_ref[...]
    ms = jnp.mean(x * x, axis=-1, keepdims=True)
    o_ref[...] = (x * lax.rsqrt(ms + RMS_EPS) * g_ref[...]).astype(o_ref.dtype)


def _rmsnorm(x, g, out_dtype, tm=1024):
    m, d = x.shape
    return pl.pallas_call(
        _rmsnorm_kernel,
        grid=(m // tm,),
        in_specs=[pl.BlockSpec((tm, d), lambda i: (i, 0)),
                  pl.BlockSpec((1, d), lambda i: (0, 0))],
        out_specs=pl.BlockSpec((tm, d), lambda i: (i, 0)),
        out_shape=jax.ShapeDtypeStruct((m, d), out_dtype),
        compiler_params=_params("arbitrary"),
        name="rmsnorm",
    )(x, g.reshape(1, d).astype(F32))


def _cast_weight_tile(w_ref, wb_ref):
    @pl.when(pl.program_id(1) == 0)
    def _():
        wb_ref[...] = w_ref[...].astype(BF16)


def _matmul_kernel(a_ref, w_ref, o_ref, wb_ref):
    _cast_weight_tile(w_ref, wb_ref)
    o_ref[...] = jnp.dot(a_ref[...], wb_ref[...],
                         preferred_element_type=F32).astype(o_ref.dtype)


def _matmul_residual_kernel(a_ref, w_ref, r_ref, o_ref, wb_ref):
    _cast_weight_tile(w_ref, wb_ref)
    o_ref[...] = (r_ref[...] + jnp.dot(a_ref[...], wb_ref[...],
                                       preferred_element_type=F32)).astype(o_ref.dtype)


def _matmul_residual_bf16_weight_kernel(a_ref, w_ref, r_ref, o_ref):
    o_ref[...] = (r_ref[...] + jnp.dot(a_ref[...], w_ref[...],
                                       preferred_element_type=F32)).astype(o_ref.dtype)


def _matmul_silu_kernel(a_ref, w_ref, o_ref, wb_ref):
    _cast_weight_tile(w_ref, wb_ref)
    x = jnp.dot(a_ref[...], wb_ref[...], preferred_element_type=F32)
    o_ref[...] = (x * _sigmoid(x)).astype(o_ref.dtype)


def _matmul_rope_kernel(a_ref, w_ref, cos_ref, sin_ref, o_ref, wb_ref, *, head_dim, key_tile0,
                        key_scale):
    _cast_weight_tile(w_ref, wb_ref)
    a = a_ref[...]
    half = head_dim // 2
    scale = jnp.where(pl.program_id(0) >= key_tile0, key_scale, 1.0)
    for h in range(o_ref.shape[1] // head_dim):
        cols = slice(h * head_dim, (h + 1) * head_dim)
        x = jnp.dot(a, wb_ref[:, cols], preferred_element_type=F32)
        swapped = jnp.concatenate([x[:, half:], x[:, :half]], axis=1)
        rotated = x * cos_ref[...] + swapped * sin_ref[...]
        o_ref[:, cols] = (rotated * scale).astype(o_ref.dtype)


def _matmul(a, w, layer, out_dtype, residual=None, *, tm, tn, name, col_start=0, n_cols=None,
            silu=False, rope=None):
    m, k = a.shape
    n = w.shape[2] - col_start if n_cols is None else n_cols
    tile0 = col_start // tn
    assert col_start % tn == 0 and n % tn == 0 and m % tm == 0
    in_specs = [pl.BlockSpec((tm, k), lambda j, i: (i, 0)),
                pl.BlockSpec((None, k, tn), lambda j, i: (layer, 0, tile0 + j))]
    args = [a, w]
    body = _matmul_kernel
    scratch = [pltpu.VMEM((k, tn), BF16)]
    if residual is not None:
        in_specs.append(pl.BlockSpec((tm, tn), lambda j, i: (i, j)))
        args.append(residual)
        body = _matmul_residual_kernel
        if w.dtype == BF16:
            body, scratch = _matmul_residual_bf16_weight_kernel, []
    elif silu:
        body = _matmul_silu_kernel
    elif rope is not None:
        head_dim = rope["cos"].shape[1]
        tiles_per_seq = rope["seq"] // tm
        assert rope["seq"] % tm == 0 and tn % head_dim == 0 and rope["key_col"] % tn == 0
        table_spec = pl.BlockSpec((tm, head_dim), lambda j, i: (i % tiles_per_seq, 0))
        in_specs += [table_spec, table_spec]
        args += [rope["cos"], rope["sin_signed"]]
        body = functools.partial(_matmul_rope_kernel, head_dim=head_dim,
                                 key_tile0=(rope["key_col"] - col_start) // tn,
                                 key_scale=rope["key_scale"])
    return pl.pallas_call(
        body,
        grid=(n // tn, m // tm),
        in_specs=in_specs,
        out_specs=pl.BlockSpec((tm, tn), lambda j, i: (i, j)),
        out_shape=jax.ShapeDtypeStruct((m, n), out_dtype),
        scratch_shapes=scratch,
        compiler_params=_params("arbitrary", "arbitrary"),
        name=name,
    )(*args)


def _matmul_blocked_transpose_kernel(a_ref, w_ref, o_ref, wt_ref, *, dh, blk):
    @pl.when(pl.program_id(1) == 0)
    def _():
        wt_ref[...] = w_ref[...].T.astype(BF16)

    out_t = lax.dot_general(wt_ref[...], a_ref[...], _NT, preferred_element_type=F32)
    for h in range(o_ref.shape[0]):
        for b in range(o_ref.shape[1]):
            o_ref[h, b] = out_t[h * dh:(h + 1) * dh, b * blk:(b + 1) * blk].astype(o_ref.dtype)


def _matmul_blocked_kernel(a_ref, w_ref, o_ref, wb_ref, *, dh, blk):
    _cast_weight_tile(w_ref, wb_ref)
    out = jnp.dot(a_ref[...], wb_ref[...], preferred_element_type=F32)
    for h in range(o_ref.shape[0]):
        for b in range(o_ref.shape[1]):
            o_ref[h, b] = out[b * blk:(b + 1) * blk, h * dh:(h + 1) * dh].astype(o_ref.dtype)


def _matmul_blocked(a, w, layer, out_dtype, *, batch, seq, dh, blk, col_start, n_cols, tm, tn,
                    name, transpose):
    m, k = a.shape
    n = n_cols
    heads, n_blocks = n // dh, seq // blk
    tile0 = col_start // tn
    tiles_per_seq = seq // tm
    assert col_start % tn == 0 and n % tn == 0 and seq % tm == 0 and tn % dh == 0 and tm % blk == 0
    body = _matmul_blocked_transpose_kernel if transpose else _matmul_blocked_kernel
    slab = (dh, blk) if transpose else (blk, dh)
    return pl.pallas_call(
        functools.partial(body, dh=dh, blk=blk),
        grid=(n // tn, m // tm),
        in_specs=[pl.BlockSpec((tm, k), lambda j, i: (i, 0)),
                  pl.BlockSpec((None, k, tn), lambda j, i: (layer, 0, tile0 + j))],
        out_specs=pl.BlockSpec((None, tn // dh, tm // blk) + slab,
                               lambda j, i: (i // tiles_per_seq, j, i % tiles_per_seq, 0, 0)),
        out_shape=jax.ShapeDtypeStruct((batch, heads, n_blocks) + slab, out_dtype),
        scratch_shapes=[pltpu.VMEM((tn, k) if transpose else (k, tn), BF16)],
        compiler_params=_params("arbitrary", "arbitrary"),
        name=name,
    )(a, w)


def _retention_kernel(lg_ref, q_ref, k_ref, v_ref, g_ref, gain_ref, o_ref,
                      state_ref, decay_ref, xi_ref, zeta_ref, *, chunk, n_chunks):
    h = pl.program_id(1)
    dv = v_ref.shape[1]
    log_gamma = lg_ref[h]

    @pl.when(pl.program_id(2) == 0)
    def _start_of_sequence():
        state_ref[...] = jnp.zeros_like(state_ref)
        r = lax.broadcasted_iota(jnp.int32, (chunk, chunk), 0)
        c = lax.broadcasted_iota(jnp.int32, (chunk, chunk), 1)
        diff = (r - c).astype(F32)
        decay_ref[...] = jnp.where(diff >= 0, jnp.exp(log_gamma * jnp.maximum(diff, 0.0)), 0.0)
        idx = lax.broadcasted_iota(jnp.int32, (chunk, dv), 0).astype(F32)
        xi_ref[...] = jnp.exp(log_gamma * (idx + 1.0))
        zeta_ref[...] = jnp.exp(log_gamma * (chunk - 1.0 - idx))

    for ci in range(n_chunks):
        rows = pl.ds(ci * chunk, chunk)
        q = q_ref[rows, :]
        k = k_ref[rows, :]
        v = v_ref[rows, :]
        scores = lax.dot_general(q, k, _NT, preferred_element_type=F32) * decay_ref[...]
        inner = jnp.dot(scores.astype(BF16), v, preferred_element_type=F32)
        state = state_ref[...]
        cross = jnp.dot(q, state.astype(BF16), preferred_element_type=F32) * xi_ref[...]
        v_decayed = (v.astype(F32) * zeta_ref[...]).astype(BF16)
        chunk_decay = xi_ref[chunk - 1:chunk, :]
        state_ref[...] = state * chunk_decay + lax.dot_general(
            k, v_decayed, _TN, preferred_element_type=F32)
        o = inner + cross
        mu = jnp.mean(o, axis=-1, keepdims=True)
        d = o - mu
        var = jnp.mean(d * d, axis=-1, keepdims=True)
        normed = d * lax.rsqrt(var + GN_EPS) * gain_ref[...]
        o_ref[rows, :] = (g_ref[rows, :].astype(F32) * normed).astype(o_ref.dtype)


def _retention_core(qk, v, gate, gn_gain, batch, seq):
    m, vwidth = v.shape
    heads = RET_HEADS
    dk = qk.shape[1] // (2 * heads)
    dv = vwidth // heads
    rows = min(RET_ROWS_PER_STEP, seq)
    chunk = RET_CHUNK
    steps = seq // rows
    assert seq % rows == 0 and rows % chunk == 0 and dk % V7X_LANES == 0 and dv % V7X_LANES == 0
    log_gamma = jnp.log1p(-jnp.power(2.0, -5.0 - jnp.arange(heads, dtype=F32)))

    row_map = lambda b, h, t: b * steps + t
    return pl.pallas_call(
        functools.partial(_retention_kernel, chunk=chunk, n_chunks=rows // chunk),
        grid=(batch, heads, steps),
        in_specs=[
            pl.BlockSpec(memory_space=pltpu.SMEM),
            pl.BlockSpec((rows, dk), lambda b, h, t: (row_map(b, h, t), h)),
            pl.BlockSpec((rows, dk), lambda b, h, t: (row_map(b, h, t), heads + h)),
            pl.BlockSpec((rows, dv), lambda b, h, t: (row_map(b, h, t), h)),
            pl.BlockSpec((rows, dv), lambda b, h, t: (row_map(b, h, t), h)),
            pl.BlockSpec((1, dv), lambda b, h, t: (0, h)),
        ],
        out_specs=pl.BlockSpec((rows, dv), lambda b, h, t: (row_map(b, h, t), h)),
        out_shape=jax.ShapeDtypeStruct((m, vwidth), BF16),
        scratch_shapes=[pltpu.VMEM((dk, dv), F32),
                        pltpu.VMEM((chunk, chunk), F32),
                        pltpu.VMEM((chunk, dv), F32),
                        pltpu.VMEM((chunk, dv), F32)],
        compiler_params=_params("arbitrary", "arbitrary", "arbitrary"),
        name="retention_core",
    )(log_gamma, qk, qk, v, gate, gn_gain.reshape(1, vwidth).astype(F32))


def _rope_tables(seq, head_dim):
    half = head_dim // 2
    inv = ROPE_BASE ** (-jnp.arange(half, dtype=F32) / half)
    ang = jnp.arange(seq).astype(F32)[:, None] * inv[None, :]
    cos = jnp.concatenate([jnp.cos(ang), jnp.cos(ang)], axis=-1)
    sin_signed = jnp.concatenate([-jnp.sin(ang), jnp.sin(ang)], axis=-1)
    return cos, sin_signed


def _t5_bucket_table(n_rel, n_buckets, max_dist):
    n = np.arange(n_rel)
    max_exact = n_buckets // 2
    nf = np.maximum(n, max_exact).astype(np.float64)
    large = max_exact + (np.log(nf / max_exact) / math.log(max_dist / max_exact)
                         * (n_buckets - max_exact)).astype(np.int64)
    large = np.minimum(large, n_buckets - 1)
    return np.where(n < max_exact, n, large).astype(np.int32)


def _moba_kernel(tbl_ref, qt_ref, qtall_ref, k_ref, vt_ref, bucket_ref, o_ref,
                 kmean_ref, bias_ref, far_bias_ref, pen_ref, s_ref, m_ref, l_ref, acc_ref,
                 *, group, n_blocks, blk, dh, topk, n_buckets, far_bucket, scale, trip_widths):
    hg = pl.program_id(1)
    qb = pl.program_id(2)
    heads = range(group)
    seq = n_blocks * blk
    slot_prev, slot_own = n_blocks - 2, n_blocks - 1
    to_log2 = math.log2(math.e)

    def cols(g):
        return slice(g * dh, (g + 1) * dh)

    @pl.when(qb == 0)
    def _start_of_heads():
        bucket_of_rel = bucket_ref[...]
        key_pos = lax.broadcasted_iota(jnp.int32, (blk, blk), 0)
        query_pos = lax.broadcasted_iota(jnp.int32, (blk, blk), 1)
        block_id = lax.broadcasted_iota(jnp.int32, (n_blocks, blk), 0)
        for g in heads:
            for j in range(n_blocks):
                kj = k_ref[g, j].astype(F32)
                kmean_ref[g, pl.ds(j, 1), :] = jnp.mean(kj, axis=0, keepdims=True)
            kmean = kmean_ref[g]
            piece0 = kmean.astype(BF16)
            rest = kmean - piece0.astype(F32)
            piece1 = rest.astype(BF16)
            piece2 = (rest - piece1.astype(F32)).astype(BF16)
            for i in range(n_blocks):
                qt_i = qtall_ref[g, i]
                gate = (jnp.dot(piece0, qt_i, preferred_element_type=F32)
                        + jnp.dot(piece1, qt_i, preferred_element_type=F32)
                        + jnp.dot(piece2, qt_i, preferred_element_type=F32))
                past = block_id < i
                gate = jnp.where(past, gate, NEG_INF)
                chosen = jnp.zeros((n_blocks, blk), jnp.bool_)
                for _ in range(topk):
                    best = jnp.max(gate, axis=0, keepdims=True)
                    first = jnp.min(jnp.where(gate == best, block_id, n_blocks), axis=0,
                                    keepdims=True)
                    pick = block_id == first
                    chosen = chosen | pick
                    gate = jnp.where(pick, -jnp.inf, gate)
                pen_ref[g, i] = jnp.where(chosen & past, 0.0, NEG_INF)

            bias_of_rel = jnp.zeros(bucket_of_rel.shape, F32)
            for b in range(n_buckets):
                bias_of_rel = jnp.where(bucket_of_rel == b, tbl_ref[hg * group + g, b], bias_of_rel)
            bias_of_rel = bias_of_rel * to_log2
            toeplitz = pltpu.roll(jnp.broadcast_to(bias_of_rel[0:1, :], (blk, 2 * blk)), 0, 1,
                                  stride=1, stride_axis=0)
            bias_ref[g, 0] = toeplitz[:, blk:]
            bias_ref[g, 1] = jnp.where(key_pos <= query_pos, toeplitz[:, :blk], NEG_INF)
            far_bias_ref[g] = jnp.full((1, blk), tbl_ref[hg * group + g, far_bucket], F32) * to_log2

    def qk(g, block):
        return jnp.dot(k_ref[g, block], qt_ref[g], preferred_element_type=F32)

    def col_max(s):
        return jnp.max(s, axis=0, keepdims=True)

    prev_block = jnp.maximum(qb - 1, 0)
    near_dots = [(qk(g, prev_block), qk(g, qb)) for g in heads]
    for g in heads:
        d_prev, d_own = near_dots[g]
        s_prev = d_prev * (scale * to_log2) + bias_ref[g, 0] + pen_ref[g, qb, pl.ds(prev_block, 1), :]
        s_own = d_own * (scale * to_log2) + bias_ref[g, 1]
        s_ref[g, slot_prev] = s_prev
        s_ref[g, slot_own] = s_own
        m_ref[g] = jnp.maximum(col_max(s_prev), col_max(s_own))

    n_far = jnp.maximum(qb - 1, 0)
    trip_plan = []
    start = 0
    for width in trip_widths[:-1]:
        n_trips = (n_far - start) // width
        trip_plan.append((width, start, n_trips))
        start = start + n_trips * width
    last = trip_widths[-1]
    trip_plan.append((last, start, (n_far - start + last - 1) // last))

    def far_blocks(start, width):
        return [(start + e, jnp.minimum(start + e, n_far - 1), start + e < n_far)
                for e in range(width)]

    def far_pass_a(blocks):
        dots = [[qk(g, block) for _, block, _ in blocks] for g in heads]
        for g in heads:
            m = m_ref[g]
            for (slot, block, real), d in zip(blocks, dots[g]):
                penalty = jnp.where(real, pen_ref[g, qb, pl.ds(block, 1), :], NEG_INF)
                s = d * (scale * to_log2) + (penalty + far_bias_ref[g])
                s_ref[g, slot] = s
                m = jnp.maximum(m, col_max(s))
            m_ref[g] = m

    def far_loops(body):
        for width, first, n_trips in trip_plan:
            def trip(i, carry, width=width, first=first):
                body(far_blocks(first + i * width, width))
                return carry

            lax.fori_loop(0, n_trips, trip, 0)

    far_loops(far_pass_a)

    def probabilities(g, slots):
        m = m_ref[g]
        ps = [jnp.exp2(s_ref[g, slot] - m) for slot in slots]
        total = ps[0].sum(axis=0, keepdims=True)
        for p in ps[1:]:
            total = total + p.sum(axis=0, keepdims=True)
        return total, [p.astype(BF16) for p in ps]

    near = [probabilities(g, (slot_prev, slot_own)) for g in heads]
    for g in heads:
        total, (p_prev, p_own) = near[g]
        l_ref[g] = total
        acc_ref[g] = (jnp.dot(vt_ref[g, prev_block], p_prev, preferred_element_type=F32)
                      + jnp.dot(vt_ref[g, qb], p_own, preferred_element_type=F32))

    def far_pass_b(blocks):
        far = [probabilities(g, [slot for slot, _, _ in blocks]) for g in heads]
        for g in heads:
            total, ps = far[g]
            l_ref[g] = l_ref[g] + total
            acc = acc_ref[g]
            for (_, block, _), p in zip(blocks, ps):
                acc = acc + jnp.dot(vt_ref[g, block], p, preferred_element_type=F32)
            acc_ref[g] = acc

    far_loops(far_pass_b)

    for g in heads:
        o_ref[:, cols(g)] = (acc_ref[g] / l_ref[g]).T.astype(o_ref.dtype)


def _moba_core(qt, k, vt, rel_bias, batch, seq):
    n_buckets, heads = rel_bias.shape
    dh = k.shape[-1]
    d_model = heads * dh
    m = batch * seq
    blk = MOBA_BLOCK
    group = MOBA_HEADS_PER_STEP
    n_blocks = seq // blk
    assert seq % blk == 0 and dh % V7X_LANES == 0 and heads % group == 0
    assert blk & (blk - 1) == 0 and n_blocks >= 2

    buckets = _t5_bucket_table(max(seq, 2 * blk), n_buckets, REL_MAX_DIST)
    far_bucket = int(buckets[blk + 1])
    assert np.all(buckets[blk + 1:] == far_bucket), "blocks two or more back must share one bucket"
    bucket_of_rel = np.broadcast_to(buckets[None, :2 * blk], (V7X_SUBLANES, 2 * blk))

    gw = group * dh
    n_groups = heads // group
    return pl.pallas_call(
        functools.partial(_moba_kernel, group=group, n_blocks=n_blocks, blk=blk, dh=dh,
                          trip_widths=MOBA_BLOCKS_PER_TRIP,
                          topk=min(MOBA_TOPK, n_blocks), n_buckets=n_buckets,
                          far_bucket=far_bucket, scale=dh ** -0.5),
        grid=(batch, n_groups, n_blocks),
        in_specs=[
            pl.BlockSpec(memory_space=pltpu.SMEM),
            pl.BlockSpec((None, group, None, dh, blk), lambda b, h, i: (b, h, i, 0, 0)),
            pl.BlockSpec((None, group, n_blocks, dh, blk), lambda b, h, i: (b, h, 0, 0, 0)),
            pl.BlockSpec((None, group, n_blocks, blk, dh), lambda b, h, i: (b, h, 0, 0, 0)),
            pl.BlockSpec((None, group, n_blocks, dh, blk), lambda b, h, i: (b, h, 0, 0, 0)),
            pl.BlockSpec((V7X_SUBLANES, 2 * blk), lambda b, h, i: (0, 0)),
        ],
        out_specs=pl.BlockSpec((blk, gw), lambda b, h, i: (b * n_blocks + i, h)),
        out_shape=jax.ShapeDtypeStruct((m, d_model), BF16),
        scratch_shapes=[pltpu.VMEM((group, n_blocks, dh), F32),
                        pltpu.VMEM((group, 2, blk, blk), F32),
                        pltpu.VMEM((group, 1, blk), F32),
                        pltpu.VMEM((group, n_blocks, n_blocks, blk), F32),
                        pltpu.VMEM((group, n_blocks, blk, blk), F32),
                        pltpu.VMEM((group, 1, blk), F32),
                        pltpu.VMEM((group, 1, blk), F32),
                        pltpu.VMEM((group, dh, blk), F32)],
        compiler_params=_params("arbitrary", "arbitrary", "arbitrary"),
        name="moba_core",
    )(rel_bias.T.astype(F32), qt, qt, k, vt, jnp.asarray(bucket_of_rel))


def _ffn_up_kernel(x_ref, wg_ref, wv_ref, cwg_ref, cwv_ref, cbg_ref, cbv_ref, o_ref,
                   wgb_ref, wvb_ref, carry_ref, *, tiles_per_seq):
    _cast_weight_tile(wg_ref, wgb_ref)
    _cast_weight_tile(wv_ref, wvb_ref)

    @pl.when(pl.program_id(1) % tiles_per_seq == 0)
    def _start_of_sequence():
        carry_ref[...] = jnp.zeros_like(carry_ref)

    x = x_ref[...]
    tm = x.shape[0]
    tn = o_ref.shape[1]
    row = lax.broadcasted_iota(jnp.int32, (tm, tn), 0)

    def conv_branch(w_ref, cw_ref, cb_ref, slot):
        u = jnp.dot(x, w_ref[...], preferred_element_type=F32)
        tail = carry_ref[slot]
        prev1 = tail[V7X_SUBLANES - 1:V7X_SUBLANES, :]
        prev2 = tail[V7X_SUBLANES - 2:V7X_SUBLANES - 1, :]
        back1 = jnp.where(row == 0, prev1, pltpu.roll(u, 1, axis=0))
        back2 = jnp.where(row == 0, prev2, jnp.where(row == 1, prev1, pltpu.roll(u, 2, axis=0)))
        carry_ref[slot] = u[tm - V7X_SUBLANES:, :]
        cw = cw_ref[...]
        return cw[2:3, :] * u + cw[1:2, :] * back1 + cw[0:1, :] * back2 + cb_ref[...]

    gate = conv_branch(wgb_ref, cwg_ref, cbg_ref, 0)
    val = conv_branch(wvb_ref, cwv_ref, cbv_ref, 1)
    o_ref[...] = (gate * _sigmoid(gate) * val).astype(o_ref.dtype)


def _ffn_up(x, w_up, conv_w, conv_b, layer, seq, *, tm, tn):
    m, k = x.shape
    d_ff = w_up.shape[2] // 2
    n_col = d_ff // tn
    assert conv_w.shape[1] == CONV_WIDTH == 3 and seq % tm == 0 and d_ff % tn == 0
    return pl.pallas_call(
        functools.partial(_ffn_up_kernel, tiles_per_seq=seq // tm),
        grid=(n_col, m // tm),
        in_specs=[
            pl.BlockSpec((tm, k), lambda j, i: (i, 0)),
            pl.BlockSpec((None, k, tn), lambda j, i: (layer, 0, j)),
            pl.BlockSpec((None, k, tn), lambda j, i: (layer, 0, n_col + j)),
            pl.BlockSpec((None, CONV_WIDTH, tn), lambda j, i: (layer, 0, j)),
            pl.BlockSpec((None, CONV_WIDTH, tn), lambda j, i: (layer, 0, n_col + j)),
            pl.BlockSpec((None, 1, tn), lambda j, i: (layer, 0, j)),
            pl.BlockSpec((None, 1, tn), lambda j, i: (layer, 0, n_col + j)),
        ],
        out_specs=pl.BlockSpec((tm, tn), lambda j, i: (i, j)),
        out_shape=jax.ShapeDtypeStruct((m, d_ff), BF16),
        scratch_shapes=[pltpu.VMEM((k, tn), BF16),
                        pltpu.VMEM((k, tn), BF16),
                        pltpu.VMEM((2, V7X_SUBLANES, tn), F32)],
        compiler_params=_params("arbitrary", "arbitrary"),
        name="ffn_up_conv_gate",
    )(x, w_up, w_up, conv_w, conv_w, conv_b, conv_b)


def kernel(x, mix_norm, ret_w_in, ret_gn, ret_w_out, moba_w_qkv, moba_w_out, rel_bias,
           ffn_norm, ffn_w_up, ffn_conv_w, ffn_conv_b, ffn_w_down, final_norm):
    batch, seq, d_model = x.shape
    depth = mix_norm.shape[0]
    conv_w = ffn_conv_w.astype(F32)
    conv_b = ffn_conv_b.astype(F32)[:, None, :]
    w_ret_out = ret_w_out.astype(BF16)
    w_moba_out = moba_w_out.astype(BF16)
    w_down = ffn_w_down.astype(BF16)
    proj = _TILES["projection"]
    h = x.reshape(batch * seq, d_model)
    for i in range(depth):
        hn = _rmsnorm(h, mix_norm[i], BF16)
        j = i // N_MIXERS
        if i % N_MIXERS == 0:
            vwidth = ret_gn.shape[1]
            dk = d_model // RET_HEADS
            cos, sin_signed = _rope_tables(seq, dk)
            rope = dict(cos=cos, sin_signed=sin_signed, seq=seq, key_col=d_model,
                        key_scale=dk ** -0.5)
            qk = _matmul(hn, ret_w_in, j, BF16, name="ret_qk_proj", n_cols=2 * d_model,
                         rope=rope, **proj)
            v = _matmul(hn, ret_w_in, j, BF16, name="ret_v_proj", col_start=2 * d_model,
                        n_cols=vwidth, **proj)
            gate = _matmul(hn, ret_w_in, j, BF16, name="ret_gate_proj",
                           col_start=2 * d_model + vwidth, n_cols=vwidth, silu=True, **proj)
            y = _retention_core(qk, v, gate, ret_gn[j], batch, seq)
            h = _matmul(y, w_ret_out, j, F32, h, name="ret_out_proj", **_TILES["ret_out_proj"])
        else:
            blocked = dict(batch=batch, seq=seq, dh=d_model // rel_bias.shape[1], blk=MOBA_BLOCK,
                           n_cols=d_model, **proj)
            qt = _matmul_blocked(hn, moba_w_qkv, j, BF16, col_start=0, name="moba_qt_proj",
                                 transpose=True, **blocked)
            k = _matmul_blocked(hn, moba_w_qkv, j, BF16, col_start=d_model, name="moba_k_proj",
                                transpose=False, **blocked)
            vt = _matmul_blocked(hn, moba_w_qkv, j, BF16, col_start=2 * d_model,
                                 name="moba_vt_proj", transpose=True, **blocked)
            o = _moba_core(qt, k, vt, rel_bias, batch, seq)
            h = _matmul(o, w_moba_out, j, F32, h, name="moba_out_proj",
                        **_TILES["moba_out_proj"])
        hn = _rmsnorm(h, ffn_norm[i], BF16)
        a = _ffn_up(hn, ffn_w_up, conv_w, conv_b, i, seq, **_TILES["ffn_up"])
        h = _matmul(a, w_down, i, F32, h, name="ffn_down_proj", **_TILES["ffn_down_proj"])
    return _rmsnorm(h, final_norm, F32).reshape(batch, seq, d_model)
```

```python
import functools
import math

import numpy as np
import jax
import jax.numpy as jnp
from jax import lax
from jax.experimental import pallas as pl
from jax.experimental.pallas import tpu as pltpu

F32 = jnp.float32
BF16 = jnp.bfloat16

N_MIXERS = 2
RET_HEADS = 8
ROPE_BASE = 10000.0
MOBA_BLOCK = 256
MOBA_TOPK = 3
REL_MAX_DIST = 128
CONV_WIDTH = 3
RMS_EPS = 1e-6
GN_EPS = 1e-5
NEG_INF = -1e30

RET_CHUNK = 256
RET_ROWS_PER_STEP = 4096
MOBA_HEADS_PER_STEP = 4
MOBA_BLOCKS_PER_TRIP = (8, 4, 2)

V7X_SUBLANES = 8
V7X_LANES = 128
V7X_VMEM_LIMIT_BYTES = 56 * 1024 * 1024

_TILES = {
    "projection": dict(tm=1024, tn=1024),
    "ret_out_proj": dict(tm=512, tn=1024),
    "moba_out_proj": dict(tm=512, tn=2048),
    "ffn_up": dict(tm=1024, tn=512),
    "ffn_down_proj": dict(tm=512, tn=1024),
}

_NT = (((1,), (1,)), ((), ()))
_TN = (((0,), (0,)), ((), ()))


def _params(*semantics):
    return pltpu.CompilerParams(dimension_semantics=semantics,
                                vmem_limit_bytes=V7X_VMEM_LIMIT_BYTES)


def _sigmoid(x):
    return 1.0 / (1.0 + jnp.exp(-x))


def _rmsnorm_kernel(x_ref, g_ref, o_ref):
    x = x_ref[...]
    ms = jnp.mean(x * x, axis=-1, keepdims=True)
    o_ref[...] = (x * lax.rsqrt(ms + RMS_EPS) * g_ref[...]).astype(o_ref.dtype)


def _rmsnorm(x, g, out_dtype, tm=1024):
    m, d = x.shape
    return pl.pallas_call(
        _rmsnorm_kernel,
        grid=(m // tm,),
        in_specs=[pl.BlockSpec((tm, d), lambda i: (i, 0)),
                  pl.BlockSpec((1, d), lambda i: (0, 0))],
        out_specs=pl.BlockSpec((tm, d), lambda i: (i, 0)),
        out_shape=jax.ShapeDtypeStruct((m, d), out_dtype),
        compiler_params=_params("arbitrary"),
        name="rmsnorm",
    )(x, g.reshape(1, d).astype(F32))


def _cast_weight_tile(w_ref, wb_ref):
    @pl.when(pl.program_id(1) == 0)
    def _():
        wb_ref[...] = w_ref[...].astype(BF16)


def _matmul_kernel(a_ref, w_ref, o_ref, wb_ref):
    _cast_weight_tile(w_ref, wb_ref)
    o_ref[...] = jnp.dot(a_ref[...], wb_ref[...],
                         preferred_element_type=F32).astype(o_ref.dtype)


def _matmul_residual_kernel(a_ref, w_ref, r_ref, o_ref, wb_ref):
    _cast_weight_tile(w_ref, wb_ref)
    o_ref[...] = (r_ref[...] + jnp.dot(a_ref[...], wb_ref[...],
                                       preferred_element_type=F32)).astype(o_ref.dtype)


def _matmul_residual_bf16_weight_kernel(a_ref, w_ref, r_ref, o_ref):
    o_ref[...] = (r_ref[...] + jnp.dot(a_ref[...], w_ref[...],
                                       preferred_element_type=F32)).astype(o_ref.dtype)


def _matmul_residual_norm_kernel(a_ref, w_ref, r_ref, g_ref, o_ref, n_ref):
    x = r_ref[...] + jnp.dot(a_ref[...], w_ref[...], preferred_element_type=F32)
    o_ref[...] = x
    ms = jnp.mean(x * x, axis=-1, keepdims=True)
    n_ref[...] = (x * lax.rsqrt(ms + RMS_EPS) * g_ref[...]).astype(n_ref.dtype)


def _matmul_silu_kernel(a_ref, w_ref, o_ref, wb_ref):
    _cast_weight_tile(w_ref, wb_ref)
    x = jnp.dot(a_ref[...], wb_ref[...], preferred_element_type=F32)
    o_ref[...] = (x * _sigmoid(x)).astype(o_ref.dtype)


def _matmul_rope_kernel(a_ref, w_ref, cos_ref, sin_ref, o_ref, wb_ref, *, head_dim, key_tile0,
                        key_scale):
    _cast_weight_tile(w_ref, wb_ref)
    a = a_ref[...]
    half = head_dim // 2
    scale = jnp.where(pl.program_id(0) >= key_tile0, key_scale, 1.0)
    for h in range(o_ref.shape[1] // head_dim):
        cols = slice(h * head_dim, (h + 1) * head_dim)
        x = jnp.dot(a, wb_ref[:, cols], preferred_element_type=F32)
        swapped = jnp.concatenate([x[:, half:], x[:, :half]], axis=1)
        rotated = x * cos_ref[...] + swapped * sin_ref[...]
        o_ref[:, cols] = (rotated * scale).astype(o_ref.dtype)


def _matmul(a, w, layer, out_dtype, residual=None, *, tm, tn, name, col_start=0, n_cols=None,
            silu=False, rope=None, norm_gain=None):
    m, k = a.shape
    n = w.shape[2] - col_start if n_cols is None else n_cols
    tile0 = col_start // tn
    assert col_start % tn == 0 and n % tn == 0 and m % tm == 0
    in_specs = [pl.BlockSpec((tm, k), lambda j, i: (i, 0)),
                pl.BlockSpec((None, k, tn), lambda j, i: (layer, 0, tile0 + j))]
    args = [a, w]
    body = _matmul_kernel
    scratch = [pltpu.VMEM((k, tn), BF16)]
    if residual is not None:
        in_specs.append(pl.BlockSpec((tm, tn), lambda j, i: (i, j)))
        args.append(residual)
        body = _matmul_residual_kernel
        if w.dtype == BF16:
            body, scratch = _matmul_residual_bf16_weight_kernel, []
        if norm_gain is not None:
            assert w.dtype == BF16 and tn == n and out_dtype == F32
            in_specs.append(pl.BlockSpec((1, n), lambda j, i: (0, 0)))
            args.append(norm_gain.reshape(1, n).astype(F32))
            row_tile = pl.BlockSpec((tm, tn), lambda j, i: (i, j))
            return pl.pallas_call(
                _matmul_residual_norm_kernel,
                grid=(1, m // tm),
                in_specs=in_specs,
                out_specs=(row_tile, row_tile),
                out_shape=(jax.ShapeDtypeStruct((m, n), F32), jax.ShapeDtypeStruct((m, n), BF16)),
                compiler_params=_params("arbitrary", "arbitrary"),
                name=name,
            )(*args)
    elif silu:
        body = _matmul_silu_kernel
    elif rope is not None:
        head_dim = rope["cos"].shape[1]
        tiles_per_seq = rope["seq"] // tm
        assert rope["seq"] % tm == 0 and tn % head_dim == 0 and rope["key_col"] % tn == 0
        table_spec = pl.BlockSpec((tm, head_dim), lambda j, i: (i % tiles_per_seq, 0))
        in_specs += [table_spec, table_spec]
        args += [rope["cos"], rope["sin_signed"]]
        body = functools.partial(_matmul_rope_kernel, head_dim=head_dim,
                                 key_tile0=(rope["key_col"] - col_start) // tn,
                                 key_scale=rope["key_scale"])
    return pl.pallas_call(
        body,
        grid=(n // tn, m // tm),
        in_specs=in_specs,
        out_specs=pl.BlockSpec((tm, tn), lambda j, i: (i, j)),
        out_shape=jax.ShapeDtypeStruct((m, n), out_dtype),
        scratch_shapes=scratch,
        compiler_params=_params("arbitrary", "arbitrary"),
        name=name,
    )(*args)


def _matmul_blocked_transpose_kernel(a_ref, w_ref, o_ref, wt_ref, *, dh, blk):
    @pl.when(pl.program_id(1) == 0)
    def _():
        wt_ref[...] = w_ref[...].T.astype(BF16)

    out_t = lax.dot_general(wt_ref[...], a_ref[...], _NT, preferred_element_type=F32)
    for h in range(o_ref.shape[0]):
        for b in range(o_ref.shape[1]):
            o_ref[h, b] = out_t[h * dh:(h + 1) * dh, b * blk:(b + 1) * blk].astype(o_ref.dtype)


def _matmul_blocked_kernel(a_ref, w_ref, o_ref, wb_ref, *, dh, blk):
    _cast_weight_tile(w_ref, wb_ref)
    out = jnp.dot(a_ref[...], wb_ref[...], preferred_element_type=F32)
    for h in range(o_ref.shape[0]):
        for b in range(o_ref.shape[1]):
            o_ref[h, b] = out[b * blk:(b + 1) * blk, h * dh:(h + 1) * dh].astype(o_ref.dtype)


def _matmul_blocked(a, w, layer, out_dtype, *, batch, seq, dh, blk, col_start, n_cols, tm, tn,
                    name, transpose):
    m, k = a.shape
    n = n_cols
    heads, n_blocks = n // dh, seq // blk
    tile0 = col_start // tn
    tiles_per_seq = seq // tm
    assert col_start % tn == 0 and n % tn == 0 and seq % tm == 0 and tn % dh == 0 and tm % blk == 0
    body = _matmul_blocked_transpose_kernel if transpose else _matmul_blocked_kernel
    slab = (dh, blk) if transpose else (blk, dh)
    return pl.pallas_call(
        functools.partial(body, dh=dh, blk=blk),
        grid=(n // tn, m // tm),
        in_specs=[pl.BlockSpec((tm, k), lambda j, i: (i, 0)),
                  pl.BlockSpec((None, k, tn), lambda j, i: (layer, 0, tile0 + j))],
        out_specs=pl.BlockSpec((None, tn // dh, tm // blk) + slab,
                               lambda j, i: (i // tiles_per_seq, j, i % tiles_per_seq, 0, 0)),
        out_shape=jax.ShapeDtypeStruct((batch, heads, n_blocks) + slab, out_dtype),
        scratch_shapes=[pltpu.VMEM((tn, k) if transpose else (k, tn), BF16)],
        compiler_params=_params("arbitrary", "arbitrary"),
        name=name,
    )(a, w)


def _retention_kernel(lg_ref, q_ref, k_ref, v_ref, g_ref, gain_ref, o_ref,
                      state_ref, decay_ref, xi_ref, zeta_ref, *, chunk, n_chunks):
    h = pl.program_id(1)
    dv = v_ref.shape[1]
    log_gamma = lg_ref[h]

    @pl.when(pl.program_id(2) == 0)
    def _start_of_sequence():
        state_ref[...] = jnp.zeros_like(state_ref)
        r = lax.broadcasted_iota(jnp.int32, (chunk, chunk), 0)
        c = lax.broadcasted_iota(jnp.int32, (chunk, chunk), 1)
        diff = (r - c).astype(F32)
        decay_ref[...] = jnp.where(diff >= 0, jnp.exp(log_gamma * jnp.maximum(diff, 0.0)), 0.0)
        idx = lax.broadcasted_iota(jnp.int32, (chunk, dv), 0).astype(F32)
        xi_ref[...] = jnp.exp(log_gamma * (idx + 1.0))
        zeta_ref[...] = jnp.exp(log_gamma * (chunk - 1.0 - idx))

    for ci in range(n_chunks):
        rows = pl.ds(ci * chunk, chunk)
        q = q_ref[rows, :]
        k = k_ref[rows, :]
        v = v_ref[rows, :]
        scores = lax.dot_general(q, k, _NT, preferred_element_type=F32) * decay_ref[...]
        inner = jnp.dot(scores.astype(BF16), v, preferred_element_type=F32)
        state = state_ref[...]
        cross = jnp.dot(q, state.astype(BF16), preferred_element_type=F32) * xi_ref[...]
        v_decayed = (v.astype(F32) * zeta_ref[...]).astype(BF16)
        chunk_decay = xi_ref[chunk - 1:chunk, :]
        state_ref[...] = state * chunk_decay + lax.dot_general(
            k, v_decayed, _TN, preferred_element_type=F32)
        o = inner + cross
        mu = jnp.mean(o, axis=-1, keepdims=True)
        d = o - mu
        var = jnp.mean(d * d, axis=-1, keepdims=True)
        normed = d * lax.rsqrt(var + GN_EPS) * gain_ref[...]
        o_ref[rows, :] = (g_ref[rows, :].astype(F32) * normed).astype(o_ref.dtype)


def _retention_core(qk, v, gate, gn_gain, batch, seq):
    m, vwidth = v.shape
    heads = RET_HEADS
    dk = qk.shape[1] // (2 * heads)
    dv = vwidth // heads
    rows = min(RET_ROWS_PER_STEP, seq)
    chunk = RET_CHUNK
    steps = seq // rows
    assert seq % rows == 0 and rows % chunk == 0 and dk % V7X_LANES == 0 and dv % V7X_LANES == 0
    log_gamma = jnp.log1p(-jnp.power(2.0, -5.0 - jnp.arange(heads, dtype=F32)))

    row_map = lambda b, h, t: b * steps + t
    return pl.pallas_call(
        functools.partial(_retention_kernel, chunk=chunk, n_chunks=rows // chunk),
        grid=(batch, heads, steps),
        in_specs=[
            pl.BlockSpec(memory_space=pltpu.SMEM),
            pl.BlockSpec((rows, dk), lambda b, h, t: (row_map(b, h, t), h)),
            pl.BlockSpec((rows, dk), lambda b, h, t: (row_map(b, h, t), heads + h)),
            pl.BlockSpec((rows, dv), lambda b, h, t: (row_map(b, h, t), h)),
            pl.BlockSpec((rows, dv), lambda b, h, t: (row_map(b, h, t), h)),
            pl.BlockSpec((1, dv), lambda b, h, t: (0, h)),
        ],
        out_specs=pl.BlockSpec((rows, dv), lambda b, h, t: (row_map(b, h, t), h)),
        out_shape=jax.ShapeDtypeStruct((m, vwidth), BF16),
        scratch_shapes=[pltpu.VMEM((dk, dv), F32),
                        pltpu.VMEM((chunk, chunk), F32),
                        pltpu.VMEM((chunk, dv), F32),
                        pltpu.VMEM((chunk, dv), F32)],
        compiler_params=_params("arbitrary", "arbitrary", "arbitrary"),
        name="retention_core",
    )(log_gamma, qk, qk, v, gate, gn_gain.reshape(1, vwidth).astype(F32))


def _rope_tables(seq, head_dim):
    half = head_dim // 2
    inv = ROPE_BASE ** (-jnp.arange(half, dtype=F32) / half)
    ang = jnp.arange(seq).astype(F32)[:, None] * inv[None, :]
    cos = jnp.concatenate([jnp.cos(ang), jnp.cos(ang)], axis=-1)
    sin_signed = jnp.concatenate([-jnp.sin(ang), jnp.sin(ang)], axis=-1)
    return cos, sin_signed


def _t5_bucket_table(n_rel, n_buckets, max_dist):
    n = np.arange(n_rel)
    max_exact = n_buckets // 2
    nf = np.maximum(n, max_exact).astype(np.float64)
    large = max_exact + (np.log(nf / max_exact) / math.log(max_dist / max_exact)
                         * (n_buckets - max_exact)).astype(np.int64)
    large = np.minimum(large, n_buckets - 1)
    return np.where(n < max_exact, n, large).astype(np.int32)


def _moba_kernel(tbl_ref, qt_ref, qtall_ref, k_ref, vt_ref, bucket_ref, o_ref,
                 kmean_ref, bias_ref, far_bias_ref, pen_ref, s_ref, m_ref, l_ref, acc_ref,
                 *, group, n_blocks, blk, dh, topk, n_buckets, far_bucket, scale, trip_widths):
    hg = pl.program_id(1)
    qb = pl.program_id(2)
    heads = range(group)
    seq = n_blocks * blk
    slot_prev, slot_own = n_blocks - 2, n_blocks - 1
    to_log2 = math.log2(math.e)

    def cols(g):
        return slice(g * dh, (g + 1) * dh)

    @pl.when(qb == 0)
    def _start_of_heads():
        bucket_of_rel = bucket_ref[...]
        key_pos = lax.broadcasted_iota(jnp.int32, (blk, blk), 0)
        query_pos = lax.broadcasted_iota(jnp.int32, (blk, blk), 1)
        block_id = lax.broadcasted_iota(jnp.int32, (n_blocks, blk), 0)
        for g in heads:
            for j in range(n_blocks):
                kj = k_ref[g, j].astype(F32)
                kmean_ref[g, pl.ds(j, 1), :] = jnp.mean(kj, axis=0, keepdims=True)
            kmean = kmean_ref[g]
            piece0 = kmean.astype(BF16)
            rest = kmean - piece0.astype(F32)
            piece1 = rest.astype(BF16)
            piece2 = (rest - piece1.astype(F32)).astype(BF16)
            for i in range(n_blocks):
                qt_i = qtall_ref[g, i]
                gate = (jnp.dot(piece0, qt_i, preferred_element_type=F32)
                        + jnp.dot(piece1, qt_i, preferred_element_type=F32)
                        + jnp.dot(piece2, qt_i, preferred_element_type=F32))
                past = block_id < i
                gate = jnp.where(past, gate, NEG_INF)
                chosen = jnp.zeros((n_blocks, blk), jnp.bool_)
                for _ in range(topk):
                    best = jnp.max(gate, axis=0, keepdims=True)
                    first = jnp.min(jnp.where(gate == best, block_id, n_blocks), axis=0,
                                    keepdims=True)
                    pick = block_id == first
                    chosen = chosen | pick
                    gate = jnp.where(pick, -jnp.inf, gate)
                pen_ref[g, i] = jnp.where(chosen & past, 0.0, NEG_INF)

            bias_of_rel = jnp.zeros(bucket_of_rel.shape, F32)
            for b in range(n_buckets):
                bias_of_rel = jnp.where(bucket_of_rel == b, tbl_ref[hg * group + g, b], bias_of_rel)
            bias_of_rel = bias_of_rel * to_log2
            toeplitz = pltpu.roll(jnp.broadcast_to(bias_of_rel[0:1, :], (blk, 2 * blk)), 0, 1,
                                  stride=1, stride_axis=0)
            bias_ref[g, 0] = toeplitz[:, blk:]
            bias_ref[g, 1] = jnp.where(key_pos <= query_pos, toeplitz[:, :blk], NEG_INF)
            far_bias_ref[g] = jnp.full((1, blk), tbl_ref[hg * group + g, far_bucket], F32) * to_log2

    def qk(g, block):
        return jnp.dot(k_ref[g, block], qt_ref[g], preferred_element_type=F32)

    def col_max(s):
        return jnp.max(s, axis=0, keepdims=True)

    prev_block = jnp.maximum(qb - 1, 0)
    near_dots = [(qk(g, prev_block), qk(g, qb)) for g in heads]
    for g in heads:
        d_prev, d_own = near_dots[g]
        s_prev = d_prev * (scale * to_log2) + bias_ref[g, 0] + pen_ref[g, qb, pl.ds(prev_block, 1), :]
        s_own = d_own * (scale * to_log2) + bias_ref[g, 1]
        s_ref[g, slot_prev] = s_prev
        s_ref[g, slot_own] = s_own
        m_ref[g] = jnp.maximum(col_max(s_prev), col_max(s_own))

    n_far = jnp.maximum(qb - 1, 0)
    trip_plan = []
    start = 0
    for width in trip_widths[:-1]:
        n_trips = (n_far - start) // width
        trip_plan.append((width, start, n_trips))
        start = start + n_trips * width
    last = trip_widths[-1]
    trip_plan.append((last, start, (n_far - start + last - 1) // last))

    def far_blocks(start, width):
        return [(start + e, jnp.minimum(start + e, n_far - 1), start + e < n_far)
                for e in range(width)]

    def far_pass_a(blocks):
        dots = [[qk(g, block) for _, block, _ in blocks] for g in heads]
        for g in heads:
            m = m_ref[g]
            for (slot, block, real), d in zip(blocks, dots[g]):
                penalty = jnp.where(real, pen_ref[g, qb, pl.ds(block, 1), :], NEG_INF)
                s = d * (scale * to_log2) + (penalty + far_bias_ref[g])
                s_ref[g, slot] = s
                m = jnp.maximum(m, col_max(s))
            m_ref[g] = m

    def far_loops(body):
        for width, first, n_trips in trip_plan:
            def trip(i, carry, width=width, first=first):
                body(far_blocks(first + i * width, width))
                return carry

            lax.fori_loop(0, n_trips, trip, 0)

    far_loops(far_pass_a)

    def probabilities(g, slots):
        m = m_ref[g]
        ps = [jnp.exp2(s_ref[g, slot] - m) for slot in slots]
        total = ps[0].sum(axis=0, keepdims=True)
        for p in ps[1:]:
            total = total + p.sum(axis=0, keepdims=True)
        return total, [p.astype(BF16) for p in ps]

    near = [probabilities(g, (slot_prev, slot_own)) for g in heads]
    for g in heads:
        total, (p_prev, p_own) = near[g]
        l_ref[g] = total
        acc_ref[g] = (jnp.dot(vt_ref[g, prev_block], p_prev, preferred_element_type=F32)
                      + jnp.dot(vt_ref[g, qb], p_own, preferred_element_type=F32))

    def far_pass_b(blocks):
        far = [probabilities(g, [slot for slot, _, _ in blocks]) for g in heads]
        for g in heads:
            total, ps = far[g]
            l_ref[g] = l_ref[g] + total
            acc = acc_ref[g]
            for (_, block, _), p in zip(blocks, ps):
                acc = acc + jnp.dot(vt_ref[g, block], p, preferred_element_type=F32)
            acc_ref[g] = acc

    far_loops(far_pass_b)

    for g in heads:
        o_ref[:, cols(g)] = (acc_ref[g] / l_ref[g]).T.astype(o_ref.dtype)


def _moba_core(qt, k, vt, rel_bias, batch, seq):
    n_buckets, heads = rel_bias.shape
    dh = k.shape[-1]
    d_model = heads * dh
    m = batch * seq
    blk = MOBA_BLOCK
    group = MOBA_HEADS_PER_STEP
    n_blocks = seq // blk
    assert seq % blk == 0 and dh % V7X_LANES == 0 and heads % group == 0
    assert blk & (blk - 1) == 0 and n_blocks >= 2

    buckets = _t5_bucket_table(max(seq, 2 * blk), n_buckets, REL_MAX_DIST)
    far_bucket = int(buckets[blk + 1])
    assert np.all(buckets[blk + 1:] == far_bucket), "blocks two or more back must share one bucket"
    bucket_of_rel = np.broadcast_to(buckets[None, :2 * blk], (V7X_SUBLANES, 2 * blk))

    gw = group * dh
    n_groups = heads // group
    return pl.pallas_call(
        functools.partial(_moba_kernel, group=group, n_blocks=n_blocks, blk=blk, dh=dh,
                          trip_widths=MOBA_BLOCKS_PER_TRIP,
                          topk=min(MOBA_TOPK, n_blocks), n_buckets=n_buckets,
                          far_bucket=far_bucket, scale=dh ** -0.5),
        grid=(batch, n_groups, n_blocks),
        in_specs=[
            pl.BlockSpec(memory_space=pltpu.SMEM),
            pl.BlockSpec((None, group, None, dh, blk), lambda b, h, i: (b, h, i, 0, 0)),
            pl.BlockSpec((None, group, n_blocks, dh, blk), lambda b, h, i: (b, h, 0, 0, 0)),
            pl.BlockSpec((None, group, n_blocks, blk, dh), lambda b, h, i: (b, h, 0, 0, 0)),
            pl.BlockSpec((None, group, n_blocks, dh, blk), lambda b, h, i: (b, h, 0, 0, 0)),
            pl.BlockSpec((V7X_SUBLANES, 2 * blk), lambda b, h, i: (0, 0)),
        ],
        out_specs=pl.BlockSpec((blk, gw), lambda b, h, i: (b * n_blocks + i, h)),
        out_shape=jax.ShapeDtypeStruct((m, d_model), BF16),
        scratch_shapes=[pltpu.VMEM((group, n_blocks, dh), F32),
                        pltpu.VMEM((group, 2, blk, blk), F32),
                        pltpu.VMEM((group, 1, blk), F32),
                        pltpu.VMEM((group, n_blocks, n_blocks, blk), F32),
                        pltpu.VMEM((group, n_blocks, blk, blk), F32),
                        pltpu.VMEM((group, 1, blk), F32),
                        pltpu.VMEM((group, 1, blk), F32),
                        pltpu.VMEM((group, dh, blk), F32)],
        compiler_params=_params("arbitrary", "arbitrary", "arbitrary"),
        name="moba_core",
    )(rel_bias.T.astype(F32), qt, qt, k, vt, jnp.asarray(bucket_of_rel))


def _ffn_up_kernel(x_ref, wg_ref, wv_ref, cwg_ref, cwv_ref, cbg_ref, cbv_ref, o_ref,
                   wgb_ref, wvb_ref, carry_ref, *, tiles_per_seq):
    _cast_weight_tile(wg_ref, wgb_ref)
    _cast_weight_tile(wv_ref, wvb_ref)

    @pl.when(pl.program_id(1) % tiles_per_seq == 0)
    def _start_of_sequence():
        carry_ref[...] = jnp.zeros_like(carry_ref)

    x = x_ref[...]
    tm = x.shape[0]
    tn = o_ref.shape[1]
    row = lax.broadcasted_iota(jnp.int32, (tm, tn), 0)

    def conv_branch(w_ref, cw_ref, cb_ref, slot):
        u = jnp.dot(x, w_ref[...], preferred_element_type=F32)
        tail = carry_ref[slot]
        prev1 = tail[V7X_SUBLANES - 1:V7X_SUBLANES, :]
        prev2 = tail[V7X_SUBLANES - 2:V7X_SUBLANES - 1, :]
        back1 = jnp.where(row == 0, prev1, pltpu.roll(u, 1, axis=0))
        back2 = jnp.where(row == 0, prev2, jnp.where(row == 1, prev1, pltpu.roll(u, 2, axis=0)))
        carry_ref[slot] = u[tm - V7X_SUBLANES:, :]
        cw = cw_ref[...]
        return cw[2:3, :] * u + cw[1:2, :] * back1 + cw[0:1, :] * back2 + cb_ref[...]

    gate = conv_branch(wgb_ref, cwg_ref, cbg_ref, 0)
    val = conv_branch(wvb_ref, cwv_ref, cbv_ref, 1)
    o_ref[...] = (gate * _sigmoid(gate) * val).astype(o_ref.dtype)


def _ffn_up(x, w_up, conv_w, conv_b, layer, seq, *, tm, tn):
    m, k = x.shape
    d_ff = w_up.shape[2] // 2
    n_col = d_ff // tn
    assert conv_w.shape[1] == CONV_WIDTH == 3 and seq % tm == 0 and d_ff % tn == 0
    return pl.pallas_call(
        functools.partial(_ffn_up_kernel, tiles_per_seq=seq // tm),
        grid=(n_col, m // tm),
        in_specs=[
            pl.BlockSpec((tm, k), lambda j, i: (i, 0)),
            pl.BlockSpec((None, k, tn), lambda j, i: (layer, 0, j)),
            pl.BlockSpec((None, k, tn), lambda j, i: (layer, 0, n_col + j)),
            pl.BlockSpec((None, CONV_WIDTH, tn), lambda j, i: (layer, 0, j)),
            pl.BlockSpec((None, CONV_WIDTH, tn), lambda j, i: (layer, 0, n_col + j)),
            pl.BlockSpec((None, 1, tn), lambda j, i: (layer, 0, j)),
            pl.BlockSpec((None, 1, tn), lambda j, i: (layer, 0, n_col + j)),
        ],
        out_specs=pl.BlockSpec((tm, tn), lambda j, i: (i, j)),
        out_shape=jax.ShapeDtypeStruct((m, d_ff), BF16),
        scratch_shapes=[pltpu.VMEM((k, tn), BF16),
                        pltpu.VMEM((k, tn), BF16),
                        pltpu.VMEM((2, V7X_SUBLANES, tn), F32)],
        compiler_params=_params("arbitrary", "arbitrary"),
        name="ffn_up_conv_gate",
    )(x, w_up, w_up, conv_w, conv_w, conv_b, conv_b)


def kernel(x, mix_norm, ret_w_in, ret_gn, ret_w_out, moba_w_qkv, moba_w_out, rel_bias,
           ffn_norm, ffn_w_up, ffn_conv_w, ffn_conv_b, ffn_w_down, final_norm):
    batch, seq, d_model = x.shape
    depth = mix_norm.shape[0]
    conv_w = ffn_conv_w.astype(F32)
    conv_b = ffn_conv_b.astype(F32)[:, None, :]
    w_ret_out = ret_w_out.astype(BF16)
    w_moba_out = moba_w_out.astype(BF16)
    w_down = ffn_w_down.astype(BF16)
    proj = _TILES["projection"]
    h = x.reshape(batch * seq, d_model)
    for i in range(depth):
        hn = _rmsnorm(h, mix_norm[i], BF16)
        j = i // N_MIXERS
        if i % N_MIXERS == 0:
            vwidth = ret_gn.shape[1]
            dk = d_model // RET_HEADS
            cos, sin_signed = _rope_tables(seq, dk)
            rope = dict(cos=cos, sin_signed=sin_signed, seq=seq, key_col=d_model,
                        key_scale=dk ** -0.5)
            qk = _matmul(hn, ret_w_in, j, BF16, name="ret_qk_proj", n_cols=2 * d_model,
                         rope=rope, **proj)
            v = _matmul(hn, ret_w_in, j, BF16, name="ret_v_proj", col_start=2 * d_model,
                        n_cols=vwidth, **proj)
            gate = _matmul(hn, ret_w_in, j, BF16, name="ret_gate_proj",
                           col_start=2 * d_model + vwidth, n_cols=vwidth, silu=True, **proj)
            y = _retention_core(qk, v, gate, ret_gn[j], batch, seq)
            h = _matmul(y, w_ret_out, j, F32, h, name="ret_out_proj", **_TILES["ret_out_proj"])
        else:
            blocked = dict(batch=batch, seq=seq, dh=d_model // rel_bias.shape[1], blk=MOBA_BLOCK,
                           n_cols=d_model, **proj)
            qt = _matmul_blocked(hn, moba_w_qkv, j, BF16, col_start=0, name="moba_qt_proj",
                                 transpose=True, **blocked)
            k = _matmul_blocked(hn, moba_w_qkv, j, BF16, col_start=d_model, name="moba_k_proj",
                                transpose=False, **blocked)
            vt = _matmul_blocked(hn, moba_w_qkv, j, BF16, col_start=2 * d_model,
                                 name="moba_vt_proj", transpose=True, **blocked)
            o = _moba_core(qt, k, vt, rel_bias, batch, seq)
            h, hn = _matmul(o, w_moba_out, j, F32, h, name="moba_out_proj",
                            norm_gain=ffn_norm[i], **_TILES["moba_out_proj"])
        if i % N_MIXERS == 0:
            hn = _rmsnorm(h, ffn_norm[i], BF16)
        a = _ffn_up(hn, ffn_w_up, conv_w, conv_b, i, seq, **_TILES["ffn_up"])
        h = _matmul(a, w_down, i, F32, h, name="ffn_down_proj", **_TILES["ffn_down_proj"])
    return _rmsnorm(h, final_norm, F32).reshape(batch, seq, d_model)
```

```python
import functools
import math

import numpy as np
import jax
import jax.numpy as jnp
from jax import lax
from jax.experimental import pallas as pl
from jax.experimental.pallas import tpu as pltpu

F32 = jnp.float32
BF16 = jnp.bfloat16

N_MIXERS = 2
RET_HEADS = 8
ROPE_BASE = 10000.0
MOBA_BLOCK = 256
MOBA_TOPK = 3
REL_MAX_DIST = 128
CONV_WIDTH = 3
RMS_EPS = 1e-6
GN_EPS = 1e-5
NEG_INF = -1e30

RET_CHUNK = 256
RET_ROWS_PER_STEP = 4096
MOBA_HEADS_PER_STEP = 4
MOBA_BLOCKS_PER_TRIP = (8, 4, 2)

V7X_SUBLANES = 8
V7X_LANES = 128
V7X_VMEM_LIMIT_BYTES = 56 * 1024 * 1024

_TILES = {
    "projection": dict(tm=1024, tn=1024),
    "ret_out_proj": dict(tm=256, tn=2048),
    "moba_out_proj": dict(tm=512, tn=2048),
    "ffn_up": dict(tm=1024, tn=512),
    "ffn_down_proj": dict(tm=512, tn=1024),
}

_NT = (((1,), (1,)), ((), ()))
_TN = (((0,), (0,)), ((), ()))


def _params(*semantics):
    return pltpu.CompilerParams(dimension_semantics=semantics,
                                vmem_limit_bytes=V7X_VMEM_LIMIT_BYTES)


def _sigmoid(x):
    return 1.0 / (1.0 + jnp.exp(-x))


def _rmsnorm_kernel(x_ref, g_ref, o_ref):
    x = x_ref[...]
    ms = jnp.mean(x * x, axis=-1, keepdims=True)
    o_ref[...] = (x * lax.rsqrt(ms + RMS_EPS) * g_ref[...]).astype(o_ref.dtype)


def _rmsnorm(x, g, out_dtype, tm=1024):
    m, d = x.shape
    return pl.pallas_call(
        _rmsnorm_kernel,
        grid=(m // tm,),
        in_specs=[pl.BlockSpec((tm, d), lambda i: (i, 0)),
                  pl.BlockSpec((1, d), lambda i: (0, 0))],
        out_specs=pl.BlockSpec((tm, d), lambda i: (i, 0)),
        out_shape=jax.ShapeDtypeStruct((m, d), out_dtype),
        compiler_params=_params("arbitrary"),
        name="rmsnorm",
    )(x, g.reshape(1, d).astype(F32))


def _cast_weight_tile(w_ref, wb_ref):
    @pl.when(pl.program_id(1) == 0)
    def _():
        wb_ref[...] = w_ref[...].astype(BF16)


def _matmul_kernel(a_ref, w_ref, o_ref, wb_ref):
    _cast_weight_tile(w_ref, wb_ref)
    o_ref[...] = jnp.dot(a_ref[...], wb_ref[...],
                         preferred_element_type=F32).astype(o_ref.dtype)


def _matmul_residual_kernel(a_ref, w_ref, r_ref, o_ref, wb_ref):
    _cast_weight_tile(w_ref, wb_ref)
    o_ref[...] = (r_ref[...] + jnp.dot(a_ref[...], wb_ref[...],
                                       preferred_element_type=F32)).astype(o_ref.dtype)


def _matmul_residual_bf16_weight_kernel(a_ref, w_ref, r_ref, o_ref):
    o_ref[...] = (r_ref[...] + jnp.dot(a_ref[...], w_ref[...],
                                       preferred_element_type=F32)).astype(o_ref.dtype)


def _matmul_residual_norm_kernel(a_ref, w_ref, r_ref, g_ref, o_ref, n_ref):
    x = r_ref[...] + jnp.dot(a_ref[...], w_ref[...], preferred_element_type=F32)
    o_ref[...] = x
    ms = jnp.mean(x * x, axis=-1, keepdims=True)
    n_ref[...] = (x * lax.rsqrt(ms + RMS_EPS) * g_ref[...]).astype(n_ref.dtype)


def _matmul_silu_kernel(a_ref, w_ref, o_ref, wb_ref):
    _cast_weight_tile(w_ref, wb_ref)
    x = jnp.dot(a_ref[...], wb_ref[...], preferred_element_type=F32)
    o_ref[...] = (x * _sigmoid(x)).astype(o_ref.dtype)


def _matmul_rope_kernel(a_ref, w_ref, cos_ref, sin_ref, o_ref, wb_ref, *, head_dim, key_tile0,
                        key_scale):
    _cast_weight_tile(w_ref, wb_ref)
    a = a_ref[...]
    half = head_dim // 2
    scale = jnp.where(pl.program_id(0) >= key_tile0, key_scale, 1.0)
    for h in range(o_ref.shape[1] // head_dim):
        cols = slice(h * head_dim, (h + 1) * head_dim)
        x = jnp.dot(a, wb_ref[:, cols], preferred_element_type=F32)
        swapped = jnp.concatenate([x[:, half:], x[:, :half]], axis=1)
        rotated = x * cos_ref[...] + swapped * sin_ref[...]
        o_ref[:, cols] = (rotated * scale).astype(o_ref.dtype)


def _matmul(a, w, layer, out_dtype, residual=None, *, tm, tn, name, col_start=0, n_cols=None,
            silu=False, rope=None, norm_gain=None):
    m, k = a.shape
    n = w.shape[2] - col_start if n_cols is None else n_cols
    tile0 = col_start // tn
    assert col_start % tn == 0 and n % tn == 0 and m % tm == 0
    in_specs = [pl.BlockSpec((tm, k), lambda j, i: (i, 0)),
                pl.BlockSpec((None, k, tn), lambda j, i: (layer, 0, tile0 + j))]
    args = [a, w]
    body = _matmul_kernel
    scratch = [pltpu.VMEM((k, tn), BF16)]
    if residual is not None:
        in_specs.append(pl.BlockSpec((tm, tn), lambda j, i: (i, j)))
        args.append(residual)
        body = _matmul_residual_kernel
        if w.dtype == BF16:
            body, scratch = _matmul_residual_bf16_weight_kernel, []
        if norm_gain is not None:
            assert w.dtype == BF16 and tn == n and out_dtype == F32
            in_specs.append(pl.BlockSpec((1, n), lambda j, i: (0, 0)))
            args.append(norm_gain.reshape(1, n).astype(F32))
            row_tile = pl.BlockSpec((tm, tn), lambda j, i: (i, j))
            return pl.pallas_call(
                _matmul_residual_norm_kernel,
                grid=(1, m // tm),
                in_specs=in_specs,
                out_specs=(row_tile, row_tile),
                out_shape=(jax.ShapeDtypeStruct((m, n), F32), jax.ShapeDtypeStruct((m, n), BF16)),
                compiler_params=_params("arbitrary", "arbitrary"),
                name=name,
            )(*args)
    elif silu:
        body = _matmul_silu_kernel
    elif rope is not None:
        head_dim = rope["cos"].shape[1]
        tiles_per_seq = rope["seq"] // tm
        assert rope["seq"] % tm == 0 and tn % head_dim == 0 and rope["key_col"] % tn == 0
        table_spec = pl.BlockSpec((tm, head_dim), lambda j, i: (i % tiles_per_seq, 0))
        in_specs += [table_spec, table_spec]
        args += [rope["cos"], rope["sin_signed"]]
        body = functools.partial(_matmul_rope_kernel, head_dim=head_dim,
                                 key_tile0=(rope["key_col"] - col_start) // tn,
                                 key_scale=rope["key_scale"])
    return pl.pallas_call(
        body,
        grid=(n // tn, m // tm),
        in_specs=in_specs,
        out_specs=pl.BlockSpec((tm, tn), lambda j, i: (i, j)),
        out_shape=jax.ShapeDtypeStruct((m, n), out_dtype),
        scratch_shapes=scratch,
        compiler_params=_params("arbitrary", "arbitrary"),
        name=name,
    )(*args)


def _matmul_blocked_transpose_kernel(a_ref, w_ref, o_ref, wt_ref, *, dh, blk):
    @pl.when(pl.program_id(1) == 0)
    def _():
        wt_ref[...] = w_ref[...].T.astype(BF16)

    out_t = lax.dot_general(wt_ref[...], a_ref[...], _NT, preferred_element_type=F32)
    for h in range(o_ref.shape[0]):
        for b in range(o_ref.shape[1]):
            o_ref[h, b] = out_t[h * dh:(h + 1) * dh, b * blk:(b + 1) * blk].astype(o_ref.dtype)


def _matmul_blocked_kernel(a_ref, w_ref, o_ref, wb_ref, *, dh, blk):
    _cast_weight_tile(w_ref, wb_ref)
    out = jnp.dot(a_ref[...], wb_ref[...], preferred_element_type=F32)
    for h in range(o_ref.shape[0]):
        for b in range(o_ref.shape[1]):
            o_ref[h, b] = out[b * blk:(b + 1) * blk, h * dh:(h + 1) * dh].astype(o_ref.dtype)


def _matmul_blocked(a, w, layer, out_dtype, *, batch, seq, dh, blk, col_start, n_cols, tm, tn,
                    name, transpose):
    m, k = a.shape
    n = n_cols
    heads, n_blocks = n // dh, seq // blk
    tile0 = col_start // tn
    tiles_per_seq = seq // tm
    assert col_start % tn == 0 and n % tn == 0 and seq % tm == 0 and tn % dh == 0 and tm % blk == 0
    body = _matmul_blocked_transpose_kernel if transpose else _matmul_blocked_kernel
    slab = (dh, blk) if transpose else (blk, dh)
    return pl.pallas_call(
        functools.partial(body, dh=dh, blk=blk),
        grid=(n // tn, m // tm),
        in_specs=[pl.BlockSpec((tm, k), lambda j, i: (i, 0)),
                  pl.BlockSpec((None, k, tn), lambda j, i: (layer, 0, tile0 + j))],
        out_specs=pl.BlockSpec((None, tn // dh, tm // blk) + slab,
                               lambda j, i: (i // tiles_per_seq, j, i % tiles_per_seq, 0, 0)),
        out_shape=jax.ShapeDtypeStruct((batch, heads, n_blocks) + slab, out_dtype),
        scratch_shapes=[pltpu.VMEM((tn, k) if transpose else (k, tn), BF16)],
        compiler_params=_params("arbitrary", "arbitrary"),
        name=name,
    )(a, w)


def _retention_kernel(lg_ref, q_ref, k_ref, v_ref, g_ref, gain_ref, o_ref,
                      state_ref, decay_ref, xi_ref, zeta_ref, *, chunk, n_chunks):
    h = pl.program_id(1)
    dv = v_ref.shape[1]
    log_gamma = lg_ref[h]

    @pl.when(pl.program_id(2) == 0)
    def _start_of_sequence():
        state_ref[...] = jnp.zeros_like(state_ref)
        r = lax.broadcasted_iota(jnp.int32, (chunk, chunk), 0)
        c = lax.broadcasted_iota(jnp.int32, (chunk, chunk), 1)
        diff = (r - c).astype(F32)
        decay_ref[...] = jnp.where(diff >= 0, jnp.exp(log_gamma * jnp.maximum(diff, 0.0)), 0.0)
        idx = lax.broadcasted_iota(jnp.int32, (chunk, dv), 0).astype(F32)
        xi_ref[...] = jnp.exp(log_gamma * (idx + 1.0))
        zeta_ref[...] = jnp.exp(log_gamma * (chunk - 1.0 - idx))

    for ci in range(n_chunks):
        rows = pl.ds(ci * chunk, chunk)
        q = q_ref[rows, :]
        k = k_ref[rows, :]
        v = v_ref[rows, :]
        scores = lax.dot_general(q, k, _NT, preferred_element_type=F32) * decay_ref[...]
        inner = jnp.dot(scores.astype(BF16), v, preferred_element_type=F32)
        state = state_ref[...]
        cross = jnp.dot(q, state.astype(BF16), preferred_element_type=F32) * xi_ref[...]
        v_decayed = (v.astype(F32) * zeta_ref[...]).astype(BF16)
        chunk_decay = xi_ref[chunk - 1:chunk, :]
        state_ref[...] = state * chunk_decay + lax.dot_general(
            k, v_decayed, _TN, preferred_element_type=F32)
        o = inner + cross
        mu = jnp.mean(o, axis=-1, keepdims=True)
        d = o - mu
        var = jnp.mean(d * d, axis=-1, keepdims=True)
        normed = d * lax.rsqrt(var + GN_EPS) * gain_ref[...]
        o_ref[rows, :] = (g_ref[rows, :].astype(F32) * normed).astype(o_ref.dtype)


def _retention_core(qk, v, gate, gn_gain, batch, seq):
    m, vwidth = v.shape
    heads = RET_HEADS
    dk = qk.shape[1] // (2 * heads)
    dv = vwidth // heads
    rows = min(RET_ROWS_PER_STEP, seq)
    chunk = RET_CHUNK
    steps = seq // rows
    assert seq % rows == 0 and rows % chunk == 0 and dk % V7X_LANES == 0 and dv % V7X_LANES == 0
    log_gamma = jnp.log1p(-jnp.power(2.0, -5.0 - jnp.arange(heads, dtype=F32)))

    row_map = lambda b, h, t: b * steps + t
    return pl.pallas_call(
        functools.partial(_retention_kernel, chunk=chunk, n_chunks=rows // chunk),
        grid=(batch, heads, steps),
        in_specs=[
            pl.BlockSpec(memory_space=pltpu.SMEM),
            pl.BlockSpec((rows, dk), lambda b, h, t: (row_map(b, h, t), h)),
            pl.BlockSpec((rows, dk), lambda b, h, t: (row_map(b, h, t), heads + h)),
            pl.BlockSpec((rows, dv), lambda b, h, t: (row_map(b, h, t), h)),
            pl.BlockSpec((rows, dv), lambda b, h, t: (row_map(b, h, t), h)),
            pl.BlockSpec((1, dv), lambda b, h, t: (0, h)),
        ],
        out_specs=pl.BlockSpec((rows, dv), lambda b, h, t: (row_map(b, h, t), h)),
        out_shape=jax.ShapeDtypeStruct((m, vwidth), BF16),
        scratch_shapes=[pltpu.VMEM((dk, dv), F32),
                        pltpu.VMEM((chunk, chunk), F32),
                        pltpu.VMEM((chunk, dv), F32),
                        pltpu.VMEM((chunk, dv), F32)],
        compiler_params=_params("arbitrary", "arbitrary", "arbitrary"),
        name="retention_core",
    )(log_gamma, qk, qk, v, gate, gn_gain.reshape(1, vwidth).astype(F32))


def _rope_tables(seq, head_dim):
    half = head_dim // 2
    inv = ROPE_BASE ** (-jnp.arange(half, dtype=F32) / half)
    ang = jnp.arange(seq).astype(F32)[:, None] * inv[None, :]
    cos = jnp.concatenate([jnp.cos(ang), jnp.cos(ang)], axis=-1)
    sin_signed = jnp.concatenate([-jnp.sin(ang), jnp.sin(ang)], axis=-1)
    return cos, sin_signed


def _t5_bucket_table(n_rel, n_buckets, max_dist):
    n = np.arange(n_rel)
    max_exact = n_buckets // 2
    nf = np.maximum(n, max_exact).astype(np.float64)
    large = max_exact + (np.log(nf / max_exact) / math.log(max_dist / max_exact)
                         * (n_buckets - max_exact)).astype(np.int64)
    large = np.minimum(large, n_buckets - 1)
    return np.where(n < max_exact, n, large).astype(np.int32)


def _moba_kernel(tbl_ref, qt_ref, qtall_ref, k_ref, vt_ref, bucket_ref, o_ref,
                 kmean_ref, bias_ref, far_bias_ref, pen_ref, s_ref, m_ref, l_ref, acc_ref,
                 *, group, n_blocks, blk, dh, topk, n_buckets, far_bucket, scale, trip_widths):
    hg = pl.program_id(1)
    qb = pl.program_id(2)
    heads = range(group)
    seq = n_blocks * blk
    slot_prev, slot_own = n_blocks - 2, n_blocks - 1
    to_log2 = math.log2(math.e)

    def cols(g):
        return slice(g * dh, (g + 1) * dh)

    @pl.when(qb == 0)
    def _start_of_heads():
        bucket_of_rel = bucket_ref[...]
        key_pos = lax.broadcasted_iota(jnp.int32, (blk, blk), 0)
        query_pos = lax.broadcasted_iota(jnp.int32, (blk, blk), 1)
        block_id = lax.broadcasted_iota(jnp.int32, (n_blocks, blk), 0)
        for g in heads:
            for j in range(n_blocks):
                kj = k_ref[g, j].astype(F32)
                kmean_ref[g, pl.ds(j, 1), :] = jnp.mean(kj, axis=0, keepdims=True)
            kmean = kmean_ref[g]
            piece0 = kmean.astype(BF16)
            rest = kmean - piece0.astype(F32)
            piece1 = rest.astype(BF16)
            piece2 = (rest - piece1.astype(F32)).astype(BF16)
            for i in range(n_blocks):
                qt_i = qtall_ref[g, i]
                gate = (jnp.dot(piece0, qt_i, preferred_element_type=F32)
                        + jnp.dot(piece1, qt_i, preferred_element_type=F32)
                        + jnp.dot(piece2, qt_i, preferred_element_type=F32))
                past = block_id < i
                gate = jnp.where(past, gate, NEG_INF)
                chosen = jnp.zeros((n_blocks, blk), jnp.bool_)
                for _ in range(topk):
                    best = jnp.max(gate, axis=0, keepdims=True)
                    first = jnp.min(jnp.where(gate == best, block_id, n_blocks), axis=0,
                                    keepdims=True)
                    pick = block_id == first
                    chosen = chosen | pick
                    gate = jnp.where(pick, -jnp.inf, gate)
                pen_ref[g, i] = jnp.where(chosen & past, 0.0, NEG_INF)

            bias_of_rel = jnp.zeros(bucket_of_rel.shape, F32)
            for b in range(n_buckets):
                bias_of_rel = jnp.where(bucket_of_rel == b, tbl_ref[hg * group + g, b], bias_of_rel)
            bias_of_rel = bias_of_rel * to_log2
            toeplitz = pltpu.roll(jnp.broadcast_to(bias_of_rel[0:1, :], (blk, 2 * blk)), 0, 1,
                                  stride=1, stride_axis=0)
            bias_ref[g, 0] = toeplitz[:, blk:]
            bias_ref[g, 1] = jnp.where(key_pos <= query_pos, toeplitz[:, :blk], NEG_INF)
            far_bias_ref[g] = jnp.full((1, blk), tbl_ref[hg * group + g, far_bucket], F32) * to_log2

    def qk(g, block):
        return jnp.dot(k_ref[g, block], qt_ref[g], preferred_element_type=F32)

    def col_max(s):
        return jnp.max(s, axis=0, keepdims=True)

    prev_block = jnp.maximum(qb - 1, 0)
    near_dots = [(qk(g, prev_block), qk(g, qb)) for g in heads]
    for g in heads:
        d_prev, d_own = near_dots[g]
        s_prev = d_prev * (scale * to_log2) + bias_ref[g, 0] + pen_ref[g, qb, pl.ds(prev_block, 1), :]
        s_own = d_own * (scale * to_log2) + bias_ref[g, 1]
        s_ref[g, slot_prev] = s_prev
        s_ref[g, slot_own] = s_own
        m_ref[g] = jnp.maximum(col_max(s_prev), col_max(s_own))

    n_far = jnp.maximum(qb - 1, 0)
    trip_plan = []
    start = 0
    for width in trip_widths[:-1]:
        n_trips = (n_far - start) // width
        trip_plan.append((width, start, n_trips))
        start = start + n_trips * width
    last = trip_widths[-1]
    trip_plan.append((last, start, (n_far - start + last - 1) // last))

    def far_blocks(start, width):
        return [(start + e, jnp.minimum(start + e, n_far - 1), start + e < n_far)
                for e in range(width)]

    def far_pass_a(blocks):
        dots = [[qk(g, block) for _, block, _ in blocks] for g in heads]
        for g in heads:
            m = m_ref[g]
            for (slot, block, real), d in zip(blocks, dots[g]):
                penalty = jnp.where(real, pen_ref[g, qb, pl.ds(block, 1), :], NEG_INF)
                s = d * (scale * to_log2) + (penalty + far_bias_ref[g])
                s_ref[g, slot] = s
                m = jnp.maximum(m, col_max(s))
            m_ref[g] = m

    def far_loops(body):
        for width, first, n_trips in trip_plan:
            def trip(i, carry, width=width, first=first):
                body(far_blocks(first + i * width, width))
                return carry

            lax.fori_loop(0, n_trips, trip, 0)

    far_loops(far_pass_a)

    def probabilities(g, slots):
        m = m_ref[g]
        ps = [jnp.exp2(s_ref[g, slot] - m) for slot in slots]
        total = ps[0].sum(axis=0, keepdims=True)
        for p in ps[1:]:
            total = total + p.sum(axis=0, keepdims=True)
        return total, [p.astype(BF16) for p in ps]

    near = [probabilities(g, (slot_prev, slot_own)) for g in heads]
    for g in heads:
        total, (p_prev, p_own) = near[g]
        l_ref[g] = total
        acc_ref[g] = (jnp.dot(vt_ref[g, prev_block], p_prev, preferred_element_type=F32)
                      + jnp.dot(vt_ref[g, qb], p_own, preferred_element_type=F32))

    def far_pass_b(blocks):
        far = [probabilities(g, [slot for slot, _, _ in blocks]) for g in heads]
        for g in heads:
            total, ps = far[g]
            l_ref[g] = l_ref[g] + total
            acc = acc_ref[g]
            for (_, block, _), p in zip(blocks, ps):
                acc = acc + jnp.dot(vt_ref[g, block], p, preferred_element_type=F32)
            acc_ref[g] = acc

    far_loops(far_pass_b)

    for g in heads:
        o_ref[:, cols(g)] = (acc_ref[g] / l_ref[g]).T.astype(o_ref.dtype)


def _moba_core(qt, k, vt, rel_bias, batch, seq):
    n_buckets, heads = rel_bias.shape
    dh = k.shape[-1]
    d_model = heads * dh
    m = batch * seq
    blk = MOBA_BLOCK
    group = MOBA_HEADS_PER_STEP
    n_blocks = seq // blk
    assert seq % blk == 0 and dh % V7X_LANES == 0 and heads % group == 0
    assert blk & (blk - 1) == 0 and n_blocks >= 2

    buckets = _t5_bucket_table(max(seq, 2 * blk), n_buckets, REL_MAX_DIST)
    far_bucket = int(buckets[blk + 1])
    assert np.all(buckets[blk + 1:] == far_bucket), "blocks two or more back must share one bucket"
    bucket_of_rel = np.broadcast_to(buckets[None, :2 * blk], (V7X_SUBLANES, 2 * blk))

    gw = group * dh
    n_groups = heads // group
    return pl.pallas_call(
        functools.partial(_moba_kernel, group=group, n_blocks=n_blocks, blk=blk, dh=dh,
                          trip_widths=MOBA_BLOCKS_PER_TRIP,
                          topk=min(MOBA_TOPK, n_blocks), n_buckets=n_buckets,
                          far_bucket=far_bucket, scale=dh ** -0.5),
        grid=(batch, n_groups, n_blocks),
        in_specs=[
            pl.BlockSpec(memory_space=pltpu.SMEM),
            pl.BlockSpec((None, group, None, dh, blk), lambda b, h, i: (b, h, i, 0, 0)),
            pl.BlockSpec((None, group, n_blocks, dh, blk), lambda b, h, i: (b, h, 0, 0, 0)),
            pl.BlockSpec((None, group, n_blocks, blk, dh), lambda b, h, i: (b, h, 0, 0, 0)),
            pl.BlockSpec((None, group, n_blocks, dh, blk), lambda b, h, i: (b, h, 0, 0, 0)),
            pl.BlockSpec((V7X_SUBLANES, 2 * blk), lambda b, h, i: (0, 0)),
        ],
        out_specs=pl.BlockSpec((blk, gw), lambda b, h, i: (b * n_blocks + i, h)),
        out_shape=jax.ShapeDtypeStruct((m, d_model), BF16),
        scratch_shapes=[pltpu.VMEM((group, n_blocks, dh), F32),
                        pltpu.VMEM((group, 2, blk, blk), F32),
                        pltpu.VMEM((group, 1, blk), F32),
                        pltpu.VMEM((group, n_blocks, n_blocks, blk), F32),
                        pltpu.VMEM((group, n_blocks, blk, blk), F32),
                        pltpu.VMEM((group, 1, blk), F32),
                        pltpu.VMEM((group, 1, blk), F32),
                        pltpu.VMEM((group, dh, blk), F32)],
        compiler_params=_params("arbitrary", "arbitrary", "arbitrary"),
        name="moba_core",
    )(rel_bias.T.astype(F32), qt, qt, k, vt, jnp.asarray(bucket_of_rel))


def _ffn_up_kernel(x_ref, wg_ref, wv_ref, cwg_ref, cwv_ref, cbg_ref, cbv_ref, o_ref,
                   wgb_ref, wvb_ref, carry_ref, *, tiles_per_seq):
    _cast_weight_tile(wg_ref, wgb_ref)
    _cast_weight_tile(wv_ref, wvb_ref)

    @pl.when(pl.program_id(1) % tiles_per_seq == 0)
    def _start_of_sequence():
        carry_ref[...] = jnp.zeros_like(carry_ref)

    x = x_ref[...]
    tm = x.shape[0]
    tn = o_ref.shape[1]
    row = lax.broadcasted_iota(jnp.int32, (tm, tn), 0)

    def conv_branch(w_ref, cw_ref, cb_ref, slot):
        u = jnp.dot(x, w_ref[...], preferred_element_type=F32)
        tail = carry_ref[slot]
        prev1 = tail[V7X_SUBLANES - 1:V7X_SUBLANES, :]
        prev2 = tail[V7X_SUBLANES - 2:V7X_SUBLANES - 1, :]
        back1 = jnp.where(row == 0, prev1, pltpu.roll(u, 1, axis=0))
        back2 = jnp.where(row == 0, prev2, jnp.where(row == 1, prev1, pltpu.roll(u, 2, axis=0)))
        carry_ref[slot] = u[tm - V7X_SUBLANES:, :]
        cw = cw_ref[...]
        return cw[2:3, :] * u + cw[1:2, :] * back1 + cw[0:1, :] * back2 + cb_ref[...]

    gate = conv_branch(wgb_ref, cwg_ref, cbg_ref, 0)
    val = conv_branch(wvb_ref, cwv_ref, cbv_ref, 1)
    o_ref[...] = (gate * _sigmoid(gate) * val).astype(o_ref.dtype)


def _ffn_up(x, w_up, conv_w, conv_b, layer, seq, *, tm, tn):
    m, k = x.shape
    d_ff = w_up.shape[2] // 2
    n_col = d_ff // tn
    assert conv_w.shape[1] == CONV_WIDTH == 3 and seq % tm == 0 and d_ff % tn == 0
    return pl.pallas_call(
        functools.partial(_ffn_up_kernel, tiles_per_seq=seq // tm),
        grid=(n_col, m // tm),
        in_specs=[
            pl.BlockSpec((tm, k), lambda j, i: (i, 0)),
            pl.BlockSpec((None, k, tn), lambda j, i: (layer, 0, j)),
            pl.BlockSpec((None, k, tn), lambda j, i: (layer, 0, n_col + j)),
            pl.BlockSpec((None, CONV_WIDTH, tn), lambda j, i: (layer, 0, j)),
            pl.BlockSpec((None, CONV_WIDTH, tn), lambda j, i: (layer, 0, n_col + j)),
            pl.BlockSpec((None, 1, tn), lambda j, i: (layer, 0, j)),
            pl.BlockSpec((None, 1, tn), lambda j, i: (layer, 0, n_col + j)),
        ],
        out_specs=pl.BlockSpec((tm, tn), lambda j, i: (i, j)),
        out_shape=jax.ShapeDtypeStruct((m, d_ff), BF16),
        scratch_shapes=[pltpu.VMEM((k, tn), BF16),
                        pltpu.VMEM((k, tn), BF16),
                        pltpu.VMEM((2, V7X_SUBLANES, tn), F32)],
        compiler_params=_params("arbitrary", "arbitrary"),
        name="ffn_up_conv_gate",
    )(x, w_up, w_up, conv_w, conv_w, conv_b, conv_b)


def kernel(x, mix_norm, ret_w_in, ret_gn, ret_w_out, moba_w_qkv, moba_w_out, rel_bias,
           ffn_norm, ffn_w_up, ffn_conv_w, ffn_conv_b, ffn_w_down, final_norm):
    batch, seq, d_model = x.shape
    depth = mix_norm.shape[0]
    conv_w = ffn_conv_w.astype(F32)
    conv_b = ffn_conv_b.astype(F32)[:, None, :]
    w_ret_out = ret_w_out.astype(BF16)
    w_moba_out = moba_w_out.astype(BF16)
    w_down = ffn_w_down.astype(BF16)
    proj = _TILES["projection"]
    h = x.reshape(batch * seq, d_model)
    for i in range(depth):
        hn = _rmsnorm(h, mix_norm[i], BF16)
        j = i // N_MIXERS
        if i % N_MIXERS == 0:
            vwidth = ret_gn.shape[1]
            dk = d_model // RET_HEADS
            cos, sin_signed = _rope_tables(seq, dk)
            rope = dict(cos=cos, sin_signed=sin_signed, seq=seq, key_col=d_model,
                        key_scale=dk ** -0.5)
            qk = _matmul(hn, ret_w_in, j, BF16, name="ret_qk_proj", n_cols=2 * d_model,
                         rope=rope, **proj)
            v = _matmul(hn, ret_w_in, j, BF16, name="ret_v_proj", col_start=2 * d_model,
                        n_cols=vwidth, **proj)
            gate = _matmul(hn, ret_w_in, j, BF16, name="ret_gate_proj",
                           col_start=2 * d_model + vwidth, n_cols=vwidth, silu=True, **proj)
            y = _retention_core(qk, v, gate, ret_gn[j], batch, seq)
            h, hn = _matmul(y, w_ret_out, j, F32, h, name="ret_out_proj",
                            norm_gain=ffn_norm[i], **_TILES["ret_out_proj"])
        else:
            blocked = dict(batch=batch, seq=seq, dh=d_model // rel_bias.shape[1], blk=MOBA_BLOCK,
                           n_cols=d_model, **proj)
            qt = _matmul_blocked(hn, moba_w_qkv, j, BF16, col_start=0, name="moba_qt_proj",
                                 transpose=True, **blocked)
            k = _matmul_blocked(hn, moba_w_qkv, j, BF16, col_start=d_model, name="moba_k_proj",
                                transpose=False, **blocked)
            vt = _matmul_blocked(hn, moba_w_qkv, j, BF16, col_start=2 * d_model,
                                 name="moba_vt_proj", transpose=True, **blocked)
            o = _moba_core(qt, k, vt, rel_bias, batch, seq)
            h, hn = _matmul(o, w_moba_out, j, F32, h, name="moba_out_proj",
                            norm_gain=ffn_norm[i], **_TILES["moba_out_proj"])
        a = _ffn_up(hn, ffn_w_up, conv_w, conv_b, i, seq, **_TILES["ffn_up"])
        h = _matmul(a, w_down, i, F32, h, name="ffn_down_proj", **_TILES["ffn_down_proj"])
    return _rmsnorm(h, final_norm, F32).reshape(batch, seq, d_model)
```
